```python
import jax, jax.numpy as jnp
from jax import lax
import numpy as np

D_MODEL = 2048
BATCH = 16
SEQ = 2048
DEPTH = 1

CHUNK = 64
LEFT_CHUNKS = 8
BAND = (LEFT_CHUNKS + 1) * CHUNK
ATT_HEAD_DIM = 64
ATT_WIDTH = D_MODEL // 2
ATT_HEADS = ATT_WIDTH // ATT_HEAD_DIM
MAX_REL = 2 * CHUNK
LRU_WIDTH = D_MODEL - ATT_WIDTH
LRU_BLOCKS = 8
LRU_BLOCK_W = LRU_WIDTH // LRU_BLOCKS
CONV_WIDTH = 4
LRU_C = 8.0
MIX_WIDTH = ATT_WIDTH + LRU_WIDTH
IN_WIDTH = 3 * ATT_WIDTH + 2 * LRU_WIDTH
MEM_LEN = 256
X_HEADS = 4
X_HEAD_DIM = 128
X_WIDTH = X_HEADS * X_HEAD_DIM
N_EXPERTS = 32
TOP_K = 4
D_FF = D_MODEL
SWIGLU_ALPHA = 1.702
SWIGLU_LIMIT = 7.0
MOE_BLOCK = 256
EPS = 1e-6

kernel_name = "hybrid_chunk_attn_rglru_moe_encoder"


def rms_norm(x, g):
    xf = x.astype(jnp.float32)
    y = xf * lax.rsqrt(jnp.mean(xf * xf, axis=-1, keepdims=True) + EPS)
    return (y * g.astype(jnp.float32)).astype(x.dtype)


def chunked_band_attention(q, k, v, rel_table):
    B, S, H, Dh = q.shape
    n_chunks = S // CHUNK
    pad = LEFT_CHUNKS * CHUNK
    k_pad = jnp.pad(k, ((0, 0), (pad, 0), (0, 0), (0, 0)))
    v_pad = jnp.pad(v, ((0, 0), (pad, 0), (0, 0), (0, 0)))
    rel = jnp.arange(CHUNK)[:, None] + pad - jnp.arange(BAND)[None, :]
    bias = rel_table[:, jnp.clip(rel, -MAX_REL, MAX_REL) + MAX_REL].astype(jnp.float32)
    q_chunks = q.reshape(B, n_chunks, CHUNK, H, Dh).transpose(1, 0, 2, 3, 4)
    scale = Dh ** -0.5

    def one_chunk(args):
        qc, c = args
        start = c * CHUNK
        kb = lax.dynamic_slice_in_dim(k_pad, start, BAND, axis=1)
        vb = lax.dynamic_slice_in_dim(v_pad, start, BAND, axis=1)
        s = jnp.einsum('bqhd,bkhd->bhqk', qc, kb).astype(jnp.float32) * scale + bias
        valid = (start - pad + jnp.arange(BAND)) >= 0
        s = jnp.where(valid, s, -jnp.inf)
        p = jax.nn.softmax(s, axis=-1).astype(vb.dtype)
        return jnp.einsum('bhqk,bkhd->bqhd', p, vb)

    out = lax.map(one_chunk, (q_chunks, jnp.arange(n_chunks)))
    return out.transpose(1, 0, 2, 3, 4).reshape(B, S, H * Dh)


def rg_lru_branch(xr, gate_in, conv_w, conv_b, w_ga, b_ga, w_gx, b_gx, lam):
    B, S, W = xr.shape
    xc = lax.conv_general_dilated(
        xr, conv_w[:, None, :], window_strides=(1,), padding=[(CONV_WIDTH - 1, 0)],
        dimension_numbers=('NWC', 'WIO', 'NWC'), feature_group_count=W) + conv_b
    xb = xc.reshape(B, S, LRU_BLOCKS, LRU_BLOCK_W)
    r = jax.nn.sigmoid(jnp.einsum('bsnc,ncd->bsnd', xb, w_ga) + b_ga).reshape(B, S, W)
    i = jax.nn.sigmoid(jnp.einsum('bsnc,ncd->bsnd', xb, w_gx) + b_gx).reshape(B, S, W)
    log_a = -LRU_C * r.astype(jnp.float32) * jax.nn.softplus(-lam.astype(jnp.float32))
    a = jnp.exp(log_a)
    b = jnp.sqrt(-jnp.expm1(2.0 * log_a)) * (i * xc).astype(jnp.float32)

    def combine(left, right):
        return (left[0] * right[0], right[0] * left[1] + right[1])

    _, h = lax.associative_scan(combine, (a, b), axis=1)
    return h.astype(xr.dtype) * jax.nn.gelu(gate_in)


def memory_cross_attention(h, mem_n, w_cq, w_ckv, w_co):
    B, S, _ = h.shape
    M = mem_n.shape[1]
    cq = (h @ w_cq).reshape(B, S, X_HEADS, X_HEAD_DIM)
    ck, cv = jnp.split(mem_n @ w_ckv, 2, axis=-1)
    ck = ck.reshape(B, M, X_HEADS, X_HEAD_DIM)
    cv = cv.reshape(B, M, X_HEADS, X_HEAD_DIM)
    s = jnp.einsum('bqhd,bkhd->bhqk', cq, ck).astype(jnp.float32) * (X_HEAD_DIM ** -0.5)
    p = jax.nn.softmax(s, axis=-1).astype(cv.dtype)
    o = jnp.einsum('bhqk,bkhd->bqhd', p, cv).reshape(B, S, X_WIDTH)
    return o @ w_co


def moe_ffn(h, w_router, b_router, w_gu, b_gu, w_down, b_down):
    B, S, D = h.shape
    n_tok = B * S
    hf = h.reshape(n_tok, D)
    logits = (hf @ w_router + b_router).astype(jnp.float32)
    top_v, top_e = lax.top_k(logits, TOP_K)
    gates = jax.nn.softmax(top_v, axis=-1)
    n_assign = n_tok * TOP_K
    flat_e = top_e.reshape(-1)
    order = jnp.argsort(flat_e, stable=True)
    sorted_e = flat_e[order]
    sorted_tok = (order // TOP_K).astype(jnp.int32)
    sorted_gate = gates.reshape(-1)[order]
    counts = jnp.bincount(flat_e, length=N_EXPERTS)
    padded = (counts + MOE_BLOCK - 1) // MOE_BLOCK * MOE_BLOCK
    pad_end = jnp.cumsum(padded)
    pad_start = pad_end - padded
    start = jnp.cumsum(counts) - counts
    dest = pad_start[sorted_e] + jnp.arange(n_assign) - start[sorted_e]
    n_blocks = (n_assign + N_EXPERTS * (MOE_BLOCK - 1) + MOE_BLOCK - 1) // MOE_BLOCK
    n_slots = n_blocks * MOE_BLOCK
    slot_tok = jnp.zeros((n_slots,), jnp.int32).at[dest].set(sorted_tok)
    slot_gate = jnp.zeros((n_slots,), jnp.float32).at[dest].set(sorted_gate)
    block_expert = jnp.minimum(
        jnp.searchsorted(pad_end, jnp.arange(n_blocks) * MOE_BLOCK, side='right'), N_EXPERTS - 1)

    def expert_block(acc, blk):
        tok, g, e = blk
        xb = hf[tok]
        gu = xb @ w_gu[e] + b_gu[e]
        gate = jnp.minimum(gu[:, :D_FF], SWIGLU_LIMIT)
        up = jnp.clip(gu[:, D_FF:], -SWIGLU_LIMIT, SWIGLU_LIMIT)
        act = (up + 1.0) * gate * jax.nn.sigmoid(SWIGLU_ALPHA * gate)
        y = act @ w_down[e] + b_down[e]
        return acc.at[tok].add(y * g[:, None].astype(y.dtype)), None

    out, _ = lax.scan(expert_block, jnp.zeros_like(hf),
                      (slot_tok.reshape(n_blocks, MOE_BLOCK),
                       slot_gate.reshape(n_blocks, MOE_BLOCK), block_expert))
    return out.reshape(B, S, D)


def setup_inputs(seed: int = 0) -> dict:
    key = jax.random.key(seed)
    ks = jax.random.split(key, 32)
    f32 = jnp.float32
    nrm = lambda k, shape, scale: jax.random.normal(k, shape, f32) * scale
    gain = lambda k, n: 1.0 + 0.05 * jax.random.normal(k, (DEPTH, n), f32)
    u = jax.random.uniform(ks[8], (DEPTH, LRU_WIDTH), f32, 0.9, 0.999)
    sig = u ** (1.0 / LRU_C)
    return {
        "x": jax.random.normal(ks[0], (BATCH, SEQ, D_MODEL), f32),
        "mem": jax.random.normal(ks[1], (BATCH, MEM_LEN, D_MODEL), f32),
        "norm_mix_g": gain(ks[2], D_MODEL),
        "w_in": nrm(ks[3], (DEPTH, D_MODEL, IN_WIDTH), D_MODEL ** -0.5),
        "rel_table": nrm(ks[4], (DEPTH, ATT_HEADS, 2 * MAX_REL + 1), 0.1),
        "conv_w": nrm(ks[5], (DEPTH, CONV_WIDTH, LRU_WIDTH), CONV_WIDTH ** -0.5),
        "conv_b": nrm(ks[6], (DEPTH, LRU_WIDTH), 0.01),
        "w_ga": nrm(ks[7], (DEPTH, LRU_BLOCKS, LRU_BLOCK_W, LRU_BLOCK_W), LRU_BLOCK_W ** -0.5),
        "b_ga": nrm(ks[9], (DEPTH, LRU_BLOCKS, LRU_BLOCK_W), 0.01),
        "w_gx": nrm(ks[10], (DEPTH, LRU_BLOCKS, LRU_BLOCK_W, LRU_BLOCK_W), LRU_BLOCK_W ** -0.5),
        "b_gx": nrm(ks[11], (DEPTH, LRU_BLOCKS, LRU_BLOCK_W), 0.01),
        "lru_lambda": jnp.log(sig / (1.0 - sig)),
        "norm_att_out_g": gain(ks[12], ATT_WIDTH),
        "norm_lru_out_g": gain(ks[13], LRU_WIDTH),
        "w_out": nrm(ks[14], (DEPTH, MIX_WIDTH, D_MODEL), MIX_WIDTH ** -0.5),
        "norm_cross_g": gain(ks[15], D_MODEL),
        "norm_mem_g": gain(ks[16], D_MODEL),
        "w_cq": nrm(ks[17], (DEPTH, D_MODEL, X_WIDTH), D_MODEL ** -0.5),
        "w_ckv": nrm(ks[18], (DEPTH, D_MODEL, 2 * X_WIDTH), D_MODEL ** -0.5),
        "w_co": nrm(ks[19], (DEPTH, X_WIDTH, D_MODEL), X_WIDTH ** -0.5),
        "norm_ffn_g": gain(ks[20], D_MODEL),
        "w_router": nrm(ks[21], (DEPTH, D_MODEL, N_EXPERTS), D_MODEL ** -0.5),
        "b_router": nrm(ks[22], (DEPTH, N_EXPERTS), 0.01),
        "w_gu": nrm(ks[23], (DEPTH, N_EXPERTS, D_MODEL, 2 * D_FF), D_MODEL ** -0.5),
        "b_gu": nrm(ks[24], (DEPTH, N_EXPERTS, 2 * D_FF), 0.01),
        "w_down": nrm(ks[25], (DEPTH, N_EXPERTS, D_FF, D_MODEL), D_FF ** -0.5),
        "b_down": nrm(ks[26], (DEPTH, N_EXPERTS, D_MODEL), 0.01),
        "norm_final_g": 1.0 + 0.05 * jax.random.normal(ks[27], (D_MODEL,), f32),
    }


def reference(x, mem, norm_mix_g, w_in, rel_table, conv_w, conv_b, w_ga, b_ga, w_gx, b_gx,
              lru_lambda, norm_att_out_g, norm_lru_out_g, w_out, norm_cross_g, norm_mem_g,
              w_cq, w_ckv, w_co, norm_ffn_g, w_router, b_router, w_gu, b_gu, w_down, b_down,
              norm_final_g):
    B, S, _ = x.shape
    splits = [ATT_WIDTH, 2 * ATT_WIDTH, 3 * ATT_WIDTH, 3 * ATT_WIDTH + LRU_WIDTH]
    for l in range(DEPTH):
        h = rms_norm(x, norm_mix_g[l])
        q, k, v, xr, yg = jnp.split(h @ w_in[l], splits, axis=-1)
        att = chunked_band_attention(q.reshape(B, S, ATT_HEADS, ATT_HEAD_DIM),
                                     k.reshape(B, S, ATT_HEADS, ATT_HEAD_DIM),
                                     v.reshape(B, S, ATT_HEADS, ATT_HEAD_DIM), rel_table[l])
        lru = rg_lru_branch(xr, yg, conv_w[l], conv_b[l], w_ga[l], b_ga[l], w_gx[l], b_gx[l],
                            lru_lambda[l])
        mixed = jnp.concatenate([rms_norm(att, norm_att_out_g[l]),
                                 rms_norm(lru, norm_lru_out_g[l])], axis=-1)
        x = x + mixed @ w_out[l]
        x = x + memory_cross_attention(rms_norm(x, norm_cross_g[l]), rms_norm(mem, norm_mem_g[l]),
                                       w_cq[l], w_ckv[l], w_co[l])
        x = x + moe_ffn(rms_norm(x, norm_ffn_g[l]), w_router[l], b_router[l], w_gu[l], b_gu[l],
                        w_down[l], b_down[l])
    return rms_norm(x, norm_final_g)
```

```python
import functools

import jax
import jax.numpy as jnp
from jax import lax
from jax.experimental import pallas as pl
from jax.experimental.pallas import tpu as pltpu

F32 = jnp.float32
BF16 = jnp.bfloat16
U32 = jnp.uint32
I32 = jnp.int32

EPS = 1e-6
CHUNK = 64
LEFT_CHUNKS = 8
ATT_HEAD_DIM = 64
MAX_REL = 2 * CHUNK
CONV_WIDTH = 4
LRU_C = 8.0
X_HEADS = 4
TOP_K = 4
SWIGLU_ALPHA = 1.702
SWIGLU_LIMIT = 7.0
NEG = -1e30

LANES = 128
VMEM_LIMIT = 56 * 1024 * 1024

PROJ_ROWS = 256
ATT_QBLOCK = 256
MID_ROWS = 256
MOE_ROWS = 512
MOE_COLS = 512
COMBINE_ROWS = 128


def _rms(x, g):
    ms = jnp.mean(x * x, axis=-1, keepdims=True)
    return x * lax.rsqrt(ms + EPS) * g


def _params(*sem):
    return pltpu.CompilerParams(dimension_semantics=sem, vmem_limit_bytes=VMEM_LIMIT)


def _norm_proj_kernel(x_ref, g_ref, w_ref, *out_refs, col_chunk):
    h = _rms(x_ref[...], g_ref[...]).astype(BF16)
    c0 = 0
    for o_ref in out_refs:
        n = o_ref.shape[-1]
        for s in range(0, n, col_chunk):
            e = min(s + col_chunk, n)
            o_ref[:, s:e] = jnp.dot(h, w_ref[:, c0 + s:c0 + e],
                                    preferred_element_type=F32).astype(o_ref.dtype)
        c0 += n


def _norm_proj(x2d, g, w_bf, outs, rows):
    m, d = x2d.shape
    n_total = w_bf.shape[1]
    assert sum(n for n, _ in outs) == n_total and m % rows == 0
    return pl.pallas_call(
        functools.partial(_norm_proj_kernel, col_chunk=512),
        grid=(m // rows,),
        in_specs=[pl.BlockSpec((rows, d), lambda i: (i, 0)),
                  pl.BlockSpec((1, d), lambda i: (0, 0)),
                  pl.BlockSpec((d, n_total), lambda i: (0, 0), pipeline_mode=pl.Buffered(1))],
        out_specs=[pl.BlockSpec((rows, n), lambda i: (i, 0)) for n, _ in outs],
        out_shape=[jax.ShapeDtypeStruct((m, n), dt) for n, dt in outs],
        compiler_params=_params("parallel"),
        name="norm_proj",
    )(x2d, g.reshape(1, d), w_bf)


def _attn_kernel(q_ref, k_ref, v_ref, bias_ref, o_ref, kpad, vpad, *, seq, pad, qb, kb):
    hd = ATT_HEAD_DIM
    kpad[0:pad, :] = jnp.zeros((pad, LANES), BF16)
    vpad[0:pad, :] = jnp.zeros((pad, LANES), BF16)
    kpad[pad:pad + seq, :] = k_ref[0]
    vpad[pad:pad + seq, :] = v_ref[0]
    scale = hd ** -0.5
    kcol = lax.broadcasted_iota(I32, (qb, kb), 1)

    def block(ib, carry):
        s0 = pl.multiple_of(ib * qb, qb)
        q = q_ref[0, pl.ds(s0, qb), :]
        kblk = kpad[pl.ds(s0, kb), :]
        vblk = vpad[pl.ds(s0, kb), :]
        before_start = kcol < (pad - s0)
        outs = []
        for hh in range(LANES // hd):
            sl = slice(hh * hd, (hh + 1) * hd)
            s = lax.dot_general(q[:, sl], kblk[:, sl], (((1,), (1,)), ((), ())),
                                preferred_element_type=F32)
            s = s * scale + bias_ref[hh]
            s = jnp.where(before_start, NEG, s)
            m = jnp.max(s, axis=-1, keepdims=True)
            p = jnp.exp(s - m)
            l = jnp.sum(p, axis=-1, keepdims=True)
            o = jnp.dot(p.astype(BF16), vblk[:, sl], preferred_element_type=F32)
            outs.append(o / l)
        o_ref[0, pl.ds(s0, qb), :] = jnp.concatenate(outs, axis=-1).astype(o_ref.dtype)
        return carry

    lax.fori_loop(0, seq // qb, block, 0)


def _attention(qkv, rel_table, att_width):
    b, seq, _ = qkv.shape
    heads = att_width // ATT_HEAD_DIM
    hp = LANES // ATT_HEAD_DIM
    qb = min(ATT_QBLOCK, seq)
    pad = LEFT_CHUNKS * CHUNK
    kb = qb + pad
    ql = jnp.arange(qb)[:, None]
    kl = jnp.arange(kb)[None, :]
    rel = ql + pad - kl
    dchunk = ql // CHUNK + LEFT_CHUNKS - kl // CHUNK
    band = (dchunk >= 0) & (dchunk <= LEFT_CHUNKS)
    bias = rel_table[:, jnp.clip(rel, -MAX_REL, MAX_REL) + MAX_REL].astype(F32)
    bias = jnp.where(band[None], bias, NEG)
    nblk = att_width // LANES
    return pl.pallas_call(
        functools.partial(_attn_kernel, seq=seq, pad=pad, qb=qb, kb=kb),
        grid=(b, heads // hp),
        in_specs=[pl.BlockSpec((1, seq, LANES), lambda i, j: (i, 0, j)),
                  pl.BlockSpec((1, seq, LANES), lambda i, j: (i, 0, nblk + j)),
                  pl.BlockSpec((1, seq, LANES), lambda i, j: (i, 0, 2 * nblk + j)),
                  pl.BlockSpec((hp, qb, kb), lambda i, j: (j, 0, 0))],
        out_specs=pl.BlockSpec((1, seq, LANES), lambda i, j: (i, 0, j)),
        out_shape=jax.ShapeDtypeStruct((b, seq, att_width), BF16),
        scratch_shapes=[pltpu.VMEM((seq + pad, LANES), BF16),
                        pltpu.VMEM((seq + pad, LANES), BF16)],
        compiler_params=_params("parallel", "parallel"),
        name="band_attention",
    )(qkv, qkv, qkv, bias)


def _lru_kernel(xr_ref, yg_ref, cw_ref, cb_ref, wg_ref, bg_ref, lam_ref, o_ref, *, seq):
    bw = xr_ref.shape[-1]
    x = xr_ref[0]
    row = lax.broadcasted_iota(I32, (seq, bw), 0)
    cw = cw_ref[...]
    xc = cb_ref[...] + cw[CONV_WIDTH - 1:CONV_WIDTH, :] * x
    for j in range(1, CONV_WIDTH):
        xs = jnp.where(row >= j, pltpu.roll(x, j, 0), 0.0)
        xc = xc + cw[CONV_WIDTH - 1 - j:CONV_WIDTH - j, :] * xs
    gates = jnp.dot(xc.astype(BF16), wg_ref[0], preferred_element_type=F32) + bg_ref[0]
    r = 1.0 / (1.0 + jnp.exp(-gates[:, :bw]))
    i = 1.0 / (1.0 + jnp.exp(-gates[:, bw:]))
    z = -lam_ref[...]
    softplus = jnp.maximum(z, 0.0) + jnp.log1p(jnp.exp(-jnp.abs(z)))
    a = jnp.exp(-LRU_C * r * softplus)
    b = jnp.sqrt(1.0 - a * a) * (i * xc)
    k = 1
    while k < seq:
        keep = row >= k
        b = jnp.where(keep, a * pltpu.roll(b, k, 0) + b, b)
        a = jnp.where(keep, a * pltpu.roll(a, k, 0), a)
        k *= 2
    g = yg_ref[0]
    gelu = 0.5 * g * (1.0 + jnp.tanh(0.7978845608028654 * (g + 0.044715 * (g * g * g))))
    o_ref[0] = (b * gelu).astype(o_ref.dtype)


def _lru(xy, conv_w, conv_b, w_ga, b_ga, w_gx, b_gx, lam, lru_width):
    b, seq, _ = xy.shape
    nb, bw, _ = w_ga.shape
    wg = jnp.concatenate([w_ga, w_gx], axis=-1).astype(BF16)
    bg = jnp.concatenate([b_ga, b_gx], axis=-1).reshape(nb, 1, 2 * bw)
    return pl.pallas_call(
        functools.partial(_lru_kernel, seq=seq),
        grid=(b, nb),
        in_specs=[pl.BlockSpec((1, seq, bw), lambda i, n: (i, 0, n)),
                  pl.BlockSpec((1, seq, bw), lambda i, n: (i, 0, nb + n)),
                  pl.BlockSpec((CONV_WIDTH, bw), lambda i, n: (0, n)),
                  pl.BlockSpec((1, bw), lambda i, n: (0, n)),
                  pl.BlockSpec((1, bw, 2 * bw), lambda i, n: (n, 0, 0)),
                  pl.BlockSpec((1, 1, 2 * bw), lambda i, n: (n, 0, 0)),
                  pl.BlockSpec((1, bw), lambda i, n: (0, n))],
        out_specs=pl.BlockSpec((1, seq, bw), lambda i, n: (i, 0, n)),
        out_shape=jax.ShapeDtypeStruct((b, seq, lru_width), BF16),
        compiler_params=_params("parallel", "parallel"),
        name="rg_lru",
    )(xy, xy, conv_w, conv_b.reshape(1, -1), wg, bg, lam.reshape(1, -1))


def _pack_bf16_pair(lo, hi):
    lo_b = pltpu.bitcast(lo.astype(BF16).astype(F32), U32) >> 16
    hi_b = pltpu.bitcast(hi.astype(BF16).astype(F32), U32) & jnp.uint32(0xFFFF0000)
    return hi_b | lo_b


def _unpack_bf16_pair(w):
    lo = pltpu.bitcast(w << 16, F32)
    hi = pltpu.bitcast(w & jnp.uint32(0xFFFF0000), F32)
    return lo, hi


def _mid_kernel(x_ref, att_ref, lru_ref, ck_ref, cv_ref, ga_ref, gl_ref, wo_ref, gc_ref, wcq_ref,
                wco_ref, gf_ref, wr_ref, br_ref,
                x2_ref, hp_ref, route_ref, gate_ref, cnt_ref, cnt_scr, *, n_experts):
    rows, d = x_ref.shape
    aw = att_ref.shape[-1]

    @pl.when((pl.program_id(0) == 0) & (pl.program_id(1) == 0))
    def _():
        cnt_scr[...] = jnp.zeros_like(cnt_scr)

    att_n = _rms(att_ref[...].astype(F32), ga_ref[...]).astype(BF16)
    lru_n = _rms(lru_ref[...].astype(F32), gl_ref[...]).astype(BF16)
    x1 = (x_ref[...]
          + jnp.dot(att_n, wo_ref[0:aw, :], preferred_element_type=F32)
          + jnp.dot(lru_n, wo_ref[aw:, :], preferred_element_type=F32))

    hq = _rms(x1, gc_ref[...]).astype(BF16)
    xw = wcq_ref.shape[-1]
    xhd = xw // X_HEADS
    cq = (jnp.dot(hq, wcq_ref[...], preferred_element_type=F32) * (xhd ** -0.5)).astype(BF16)
    ck = ck_ref[0]
    cv = cv_ref[0]
    heads = []
    for h in range(X_HEADS):
        sl = slice(h * xhd, (h + 1) * xhd)
        s = lax.dot_general(cq[:, sl], ck[:, sl], (((1,), (1,)), ((), ())),
                            preferred_element_type=F32)
        m = jnp.max(s, axis=-1, keepdims=True)
        p = jnp.exp(s - m)
        l = jnp.sum(p, axis=-1, keepdims=True)
        heads.append(jnp.dot(p.astype(BF16), cv[:, sl], preferred_element_type=F32) / l)
    o = jnp.concatenate(heads, axis=-1).astype(BF16)
    x2 = x1 + jnp.dot(o, wco_ref[...], preferred_element_type=F32)
    x2_ref[...] = x2

    hn = _rms(x2, gf_ref[...])
    hp_ref[...] = _pack_bf16_pair(hn[:, :d // 2], hn[:, d // 2:])

    logits = jnp.dot(hn.astype(BF16), wr_ref[...], preferred_element_type=F32) + br_ref[...]
    lane = lax.broadcasted_iota(I32, (rows, LANES), 1).astype(F32)
    work = logits
    sel_e, sel_v = [], []
    onehot = jnp.zeros((rows, LANES), F32)
    for _ in range(TOP_K):
        v = jnp.max(work, axis=-1, keepdims=True)
        e = jnp.min(jnp.where(work == v, lane, float(LANES)), axis=-1, keepdims=True)
        hit = lane == e
        onehot = jnp.where(hit, 1.0, onehot)
        work = jnp.where(hit, NEG * 2, work)
        sel_e.append(e)
        sel_v.append(v)
    ex = [jnp.exp(v - sel_v[0]) for v in sel_v]
    den = ex[0] + ex[1] + ex[2] + ex[3]

    ri = lax.broadcasted_iota(I32, (rows, rows), 0)
    ci = lax.broadcasted_iota(I32, (rows, rows), 1)
    tri = jnp.where(ci < ri, 1.0, 0.0).astype(BF16)
    before = jnp.dot(tri, onehot.astype(BF16), preferred_element_type=F32) + cnt_scr[...]
    route = jnp.zeros((rows, LANES), F32)
    gate = jnp.zeros((rows, LANES), F32)
    for k in range(TOP_K):
        rank = jnp.sum(jnp.where(lane == sel_e[k], before, 0.0), axis=-1, keepdims=True)
        route = jnp.where(lane == float(k), sel_e[k], route)
        route = jnp.where(lane == float(TOP_K + k), rank, route)
        gate = jnp.where(lane == float(k), ex[k] / den, gate)
    route_ref[...] = route.astype(I32)
    gate_ref[...] = gate
    cnt_scr[...] = cnt_scr[...] + jnp.sum(onehot, axis=0, keepdims=True)
    cnt_ref[...] = cnt_scr[...]


def _mid(x, att, lru, ckv, ga, gl, wo_bf, gc, wcq_bf, wco_bf, gf, wr_pad, br_pad, n_experts, rows):
    b, seq, d = x.shape
    aw, lw = att.shape[-1], lru.shape[-1]
    mem_len, xw2 = ckv.shape[1], ckv.shape[2]
    xw = xw2 // 2
    n = b * seq
    nt = seq // rows
    row_map = lambda i, t: (i * nt + t, 0)
    const = lambda i, t: (0, 0)
    res = lambda shape: pl.BlockSpec(shape, const, pipeline_mode=pl.Buffered(1))
    return pl.pallas_call(
        functools.partial(_mid_kernel, n_experts=n_experts),
        grid=(b, nt),
        in_specs=[pl.BlockSpec((rows, d), row_map),
                  pl.BlockSpec((rows, aw), row_map),
                  pl.BlockSpec((rows, lw), row_map),
                  pl.BlockSpec((1, mem_len, xw), lambda i, t: (i, 0, 0)),
                  pl.BlockSpec((1, mem_len, xw), lambda i, t: (i, 0, 1)),
                  res((1, aw)), res((1, lw)), res((aw + lw, d)), res((1, d)), res((d, xw)),
                  res((xw, d)), res((1, d)), res((d, LANES)), res((1, LANES))],
        out_specs=[pl.BlockSpec((rows, d), row_map),
                   pl.BlockSpec((rows, d // 2), row_map),
                   pl.BlockSpec((rows, LANES), row_map),
                   pl.BlockSpec((rows, LANES), row_map),
                   pl.BlockSpec((1, LANES), const)],
        out_shape=[jax.ShapeDtypeStruct((n, d), F32),
                   jax.ShapeDtypeStruct((n, d // 2), U32),
                   jax.ShapeDtypeStruct((n, LANES), I32),
                   jax.ShapeDtypeStruct((n, LANES), F32),
                   jax.ShapeDtypeStruct((1, LANES), F32)],
        scratch_shapes=[pltpu.VMEM((1, LANES), F32)],
        compiler_params=_params("arbitrary", "arbitrary"),
        name="mix_cross_router",
    )(x.reshape(n, d), att.reshape(n, aw), lru.reshape(n, lw), ckv, ckv,
      ga.reshape(1, aw), gl.reshape(1, lw), wo_bf, gc.reshape(1, d), wcq_bf, wco_bf,
      gf.reshape(1, d), wr_pad, br_pad)


def _gather_kernel(meta_ref, idx_ref, src_ref, o_ref, sem, *, rows):
    i = pl.program_id(0)

    def row_copy(r):
        tok = idx_ref[0, 0, r]
        return pltpu.make_async_copy(src_ref.at[pl.ds(tok, 1), :], o_ref.at[pl.ds(r, 1), :], sem)

    @pl.when(i < meta_ref[0])
    def _():
        def start(r, c):
            row_copy(r).start()
            return c

        def wait(r, c):
            row_copy(r).wait()
            return c

        lax.fori_loop(0, rows, start, 0, unroll=8)
        lax.fori_loop(0, rows, wait, 0, unroll=8)

    @pl.when(i >= meta_ref[0])
    def _():
        o_ref[...] = jnp.zeros_like(o_ref)


def _gather_rows(meta, slot_tok, src, rows):
    n_tiles = slot_tok.shape[0] // rows
    w = src.shape[1]
    used = lambda i, meta: jnp.minimum(i, meta[0] - 1)
    return pl.pallas_call(
        functools.partial(_gather_kernel, rows=rows),
        grid_spec=pltpu.PrefetchScalarGridSpec(
            num_scalar_prefetch=1,
            grid=(n_tiles,),
            in_specs=[pl.BlockSpec((1, 1, rows), lambda i, meta: (used(i, meta), 0, 0),
                                   memory_space=pltpu.SMEM),
                      pl.BlockSpec(memory_space=pl.ANY)],
            out_specs=pl.BlockSpec((rows, w), lambda i, meta: (i, 0)),
            scratch_shapes=[pltpu.SemaphoreType.DMA(())]),
        out_shape=jax.ShapeDtypeStruct((n_tiles * rows, w), src.dtype),
        compiler_params=_params("arbitrary"),
        name="dispatch_gather",
    )(meta, slot_tok.reshape(n_tiles, 1, rows), src)


def _grouped_kernel(te_ref, meta_ref, x_ref, wa_ref, wb_ref, ba_ref, bb_ref, o_ref, wa_bf, wb_bf,
                    *, swiglu):
    i = pl.program_id(1)
    prev = te_ref[jnp.maximum(i - 1, 0)]
    new_expert = (i == 0) | (te_ref[i] != prev)

    @pl.when(new_expert)
    def _():
        wa_bf[...] = wa_ref[0].astype(BF16)
        wb_bf[...] = wb_ref[0].astype(BF16)

    @pl.when(i < meta_ref[0])
    def _():
        if swiglu:
            lo, hi = _unpack_bf16_pair(x_ref[...])
            xb = jnp.concatenate([lo.astype(BF16), hi.astype(BF16)], axis=-1)
        else:
            xb = x_ref[...]
        a = jnp.dot(xb, wa_bf[...], preferred_element_type=F32) + ba_ref[0]
        b = jnp.dot(xb, wb_bf[...], preferred_element_type=F32) + bb_ref[0]
        if swiglu:
            gate = jnp.minimum(a, SWIGLU_LIMIT)
            up = jnp.clip(b, -SWIGLU_LIMIT, SWIGLU_LIMIT)
            act = (up + 1.0) * gate * (1.0 / (1.0 + jnp.exp(-SWIGLU_ALPHA * gate)))
            o_ref[...] = act.astype(o_ref.dtype)
        else:
            o_ref[...] = _pack_bf16_pair(a, b)

    @pl.when(i >= meta_ref[0])
    def _():
        o_ref[...] = jnp.zeros_like(o_ref)


def _grouped_matmul(tile_expert, meta, x, w, bias, rows, cols, swiglu):
    n_slots = x.shape[0]
    n_tiles = n_slots // rows
    n_exp, kdim, n2 = w.shape
    half = n2 // 2
    nj = half // cols
    used = lambda i, meta: jnp.minimum(i, meta[0] - 1)
    out_dtype = BF16 if swiglu else U32
    return pl.pallas_call(
        functools.partial(_grouped_kernel, swiglu=swiglu),
        grid_spec=pltpu.PrefetchScalarGridSpec(
            num_scalar_prefetch=2,
            grid=(nj, n_tiles),
            in_specs=[pl.BlockSpec((rows, x.shape[1]), lambda j, i, te, meta: (used(i, meta), 0)),
                      pl.BlockSpec((1, kdim, cols), lambda j, i, te, meta: (te[i], 0, j)),
                      pl.BlockSpec((1, kdim, cols), lambda j, i, te, meta: (te[i], 0, nj + j)),
                      pl.BlockSpec((1, 1, cols), lambda j, i, te, meta: (te[i], 0, j)),
                      pl.BlockSpec((1, 1, cols), lambda j, i, te, meta: (te[i], 0, nj + j))],
            out_specs=pl.BlockSpec((rows, cols), lambda j, i, te, meta: (i, j)),
            scratch_shapes=[pltpu.VMEM((kdim, cols), BF16), pltpu.VMEM((kdim, cols), BF16)]),
        out_shape=jax.ShapeDtypeStruct((n_slots, half), out_dtype),
        compiler_params=_params("arbitrary", "arbitrary"),
        name="moe_gate_up" if swiglu else "moe_down",
    )(tile_expert, meta, x, w, w, bias.reshape(n_exp, 1, n2), bias.reshape(n_exp, 1, n2))


def _combine_kernel(dest_ref, y_ref, x2_ref, gate_ref, g_ref, o_ref, ybuf, sem, *, rows):
    def row_copy(r, k):
        slot = dest_ref[0, 0, r * TOP_K + k]
        return pltpu.make_async_copy(y_ref.at[pl.ds(slot, 1), :], ybuf.at[k, pl.ds(r, 1), :], sem)

    def start(r, c):
        for k in range(TOP_K):
            row_copy(r, k).start()
        return c

    def wait(r, c):
        for k in range(TOP_K):
            row_copy(r, k).wait()
        return c

    lax.fori_loop(0, rows, start, 0, unroll=4)
    lax.fori_loop(0, rows, wait, 0, unroll=4)
    gate = gate_ref[...]
    half = ybuf.shape[-1]
    acc_lo = x2_ref[:, :half]
    acc_hi = x2_ref[:, half:]
    for k in range(TOP_K):
        lo, hi = _unpack_bf16_pair(ybuf[k])
        gk = gate[:, k:k + 1]
        acc_lo = acc_lo + gk * lo
        acc_hi = acc_hi + gk * hi
    ms = (jnp.sum(acc_lo * acc_lo, axis=-1, keepdims=True)
          + jnp.sum(acc_hi * acc_hi, axis=-1, keepdims=True)) / (2 * half)
    inv = lax.rsqrt(ms + EPS)
    o_ref[:, :half] = acc_lo * inv * g_ref[:, :half]
    o_ref[:, half:] = acc_hi * inv * g_ref[:, half:]


def _combine(dest, y, x2, gate, g, rows):
    n, d = x2.shape
    nt = n // rows
    return pl.pallas_call(
        functools.partial(_combine_kernel, rows=rows),
        grid=(nt,),
        in_specs=[pl.BlockSpec((1, 1, rows * TOP_K), lambda i: (i, 0, 0), memory_space=pltpu.SMEM),
                  pl.BlockSpec(memory_space=pl.ANY),
                  pl.BlockSpec((rows, d), lambda i: (i, 0)),
                  pl.BlockSpec((rows, LANES), lambda i: (i, 0)),
                  pl.BlockSpec((1, d), lambda i: (0, 0))],
        out_specs=pl.BlockSpec((rows, d), lambda i: (i, 0)),
        out_shape=jax.ShapeDtypeStruct((n, d), F32),
        scratch_shapes=[pltpu.VMEM((TOP_K, rows, d // 2), U32), pltpu.SemaphoreType.DMA(())],
        compiler_params=_params("arbitrary"),
        name="combine_norm",
    )(dest.reshape(nt, 1, rows * TOP_K), y, x2, gate, g.reshape(1, d))


def kernel(x, mem, norm_mix_g, w_in, rel_table, conv_w, conv_b, w_ga, b_ga, w_gx, b_gx, lru_lambda, norm_att_out_g, norm_lru_out_g, w_out, norm_cross_g, norm_mem_g, w_cq, w_ckv, w_co, norm_ffn_g, w_router, b_router, w_gu, b_gu, w_down, b_down, norm_final_g):
    b, seq, d = x.shape
    depth = w_in.shape[0]
    att_width = d // 2
    lru_width = d - att_width
    n_experts = w_router.shape[-1]
    n_tok = b * seq
    mem_len = mem.shape[1]
    assert depth == 1 and n_experts <= LANES and seq % min(ATT_QBLOCK, seq) == 0

    moe_rows = min(MOE_ROWS, n_tok)
    n_tiles = (n_tok * TOP_K + n_experts * (moe_rows - 1)) // moe_rows
    n_slots = n_tiles * moe_rows

    for l in range(depth):
        qkv, xy = _norm_proj(x.reshape(n_tok, d), norm_mix_g[l], w_in[l].astype(BF16),
                             [(3 * att_width, BF16), (2 * lru_width, F32)], min(PROJ_ROWS, seq))
        att = _attention(qkv.reshape(b, seq, 3 * att_width), rel_table[l], att_width)
        lru = _lru(xy.reshape(b, seq, 2 * lru_width), conv_w[l], conv_b[l], w_ga[l], b_ga[l],
                   w_gx[l], b_gx[l], lru_lambda[l], lru_width)
        (ckv,) = _norm_proj(mem.reshape(b * mem_len, d), norm_mem_g[l], w_ckv[l].astype(BF16),
                            [(w_ckv.shape[-1], BF16)], min(PROJ_ROWS, mem_len))
        ckv = ckv.reshape(b, mem_len, -1)
        wr_pad = jnp.zeros((d, LANES), F32).at[:, :n_experts].set(w_router[l]).astype(BF16)
        br_pad = jnp.full((1, LANES), NEG, F32).at[0, :n_experts].set(b_router[l])
        x2, hpack, route, gate, counts = _mid(
            x, att, lru, ckv, norm_att_out_g[l], norm_lru_out_g[l], w_out[l].astype(BF16),
            norm_cross_g[l], w_cq[l].astype(BF16), w_co[l].astype(BF16), norm_ffn_g[l],
            wr_pad, br_pad, n_experts, min(MID_ROWS, seq))
        counts = counts[0, :n_experts].astype(I32)
        padded = (counts + moe_rows - 1) // moe_rows * moe_rows
        pad_end = jnp.cumsum(padded)
        pad_start = pad_end - padded
        sel_e = route[:, :TOP_K]
        dest = pad_start[sel_e] + route[:, TOP_K:2 * TOP_K]
        tok_ids = jnp.broadcast_to(jnp.arange(n_tok, dtype=I32)[:, None], (n_tok, TOP_K))
        slot_tok = jnp.zeros((n_slots,), I32).at[dest.reshape(-1)].set(tok_ids.reshape(-1))
        tile_start = jnp.arange(n_tiles, dtype=I32) * moe_rows
        n_used = (pad_end[-1] // moe_rows).astype(I32)
        tile_expert = jnp.searchsorted(pad_end, jnp.minimum(tile_start, pad_end[-1] - 1),
                                       side='right').astype(I32)
        tile_expert = jnp.minimum(tile_expert, n_experts - 1)
        meta = n_used.reshape(1)
        xs = _gather_rows(meta, slot_tok, hpack, moe_rows)
        cols = min(MOE_COLS, w_down.shape[2] // 2)
        act = _grouped_matmul(tile_expert, meta, xs, w_gu[l], b_gu[l], moe_rows, cols, True)
        y = _grouped_matmul(tile_expert, meta, act, w_down[l], b_down[l], moe_rows,
                            min(MOE_COLS, d // 2), False)
        x = _combine(dest, y, x2, gate, norm_final_g, min(COMBINE_ROWS, n_tok)).reshape(b, seq, d)
    return x
```

```python
import functools

import jax
import jax.numpy as jnp
from jax import lax
from jax.experimental import pallas as pl
from jax.experimental.pallas import tpu as pltpu

F32 = jnp.float32
BF16 = jnp.bfloat16
U32 = jnp.uint32
I32 = jnp.int32

EPS = 1e-6
CHUNK = 64
LEFT_CHUNKS = 8
ATT_HEAD_DIM = 64
MAX_REL = 2 * CHUNK
CONV_WIDTH = 4
LRU_C = 8.0
X_HEADS = 4
TOP_K = 4
SWIGLU_ALPHA = 1.702
SWIGLU_LIMIT = 7.0
NEG = -1e30

LANES = 128
SUBLANES = 8
VMEM_LIMIT = 56 * 1024 * 1024

PROJ_ROWS = 256
ATT_QBLOCK = 256
MID_ROWS = 256
MOE_ROWS = 512
MOE_COLS = 512
DISPATCH_ROWS = 256
COMBINE_ROWS = 256


def _rms(x, g):
    ms = jnp.mean(x * x, axis=-1, keepdims=True)
    return x * lax.rsqrt(ms + EPS) * g


def _params(*sem):
    return pltpu.CompilerParams(dimension_semantics=sem, vmem_limit_bytes=VMEM_LIMIT)


def _norm_proj_kernel(x_ref, g_ref, w_ref, *out_refs, col_chunk):
    h = _rms(x_ref[...], g_ref[...]).astype(BF16)
    c0 = 0
    for o_ref in out_refs:
        n = o_ref.shape[-1]
        for s in range(0, n, col_chunk):
            e = min(s + col_chunk, n)
            o_ref[:, s:e] = jnp.dot(h, w_ref[:, c0 + s:c0 + e],
                                    preferred_element_type=F32).astype(o_ref.dtype)
        c0 += n


def _norm_proj(x2d, g, w_bf, outs, rows):
    m, d = x2d.shape
    n_total = w_bf.shape[1]
    assert sum(n for n, _ in outs) == n_total and m % rows == 0
    return pl.pallas_call(
        functools.partial(_norm_proj_kernel, col_chunk=512),
        grid=(m // rows,),
        in_specs=[pl.BlockSpec((rows, d), lambda i: (i, 0)),
                  pl.BlockSpec((1, d), lambda i: (0, 0)),
                  pl.BlockSpec((d, n_total), lambda i: (0, 0), pipeline_mode=pl.Buffered(1))],
        out_specs=[pl.BlockSpec((rows, n), lambda i: (i, 0)) for n, _ in outs],
        out_shape=[jax.ShapeDtypeStruct((m, n), dt) for n, dt in outs],
        compiler_params=_params("parallel"),
        name="norm_proj",
    )(x2d, g.reshape(1, d), w_bf)


def _attn_kernel(q_ref, k_ref, v_ref, bias_ref, o_ref, kpad, vpad, *, seq, pad, qb, kb):
    hd = ATT_HEAD_DIM
    kpad[0:pad, :] = jnp.zeros((pad, LANES), BF16)
    vpad[0:pad, :] = jnp.zeros((pad, LANES), BF16)
    kpad[pad:pad + seq, :] = k_ref[0]
    vpad[pad:pad + seq, :] = v_ref[0]
    scale = hd ** -0.5
    kcol = lax.broadcasted_iota(I32, (qb, kb), 1)

    def block(ib, carry):
        s0 = pl.multiple_of(ib * qb, qb)
        q = q_ref[0, pl.ds(s0, qb), :]
        kblk = kpad[pl.ds(s0, kb), :]
        vblk = vpad[pl.ds(s0, kb), :]
        before_start = kcol < (pad - s0)
        outs = []
        for hh in range(LANES // hd):
            sl = slice(hh * hd, (hh + 1) * hd)
            s = lax.dot_general(q[:, sl], kblk[:, sl], (((1,), (1,)), ((), ())),
                                preferred_element_type=F32)
            s = s * scale + bias_ref[hh]
            s = jnp.where(before_start, NEG, s)
            m = jnp.max(s, axis=-1, keepdims=True)
            p = jnp.exp(s - m)
            l = jnp.sum(p, axis=-1, keepdims=True)
            o = jnp.dot(p.astype(BF16), vblk[:, sl], preferred_element_type=F32)
            outs.append(o / l)
        o_ref[0, pl.ds(s0, qb), :] = jnp.concatenate(outs, axis=-1).astype(o_ref.dtype)
        return carry

    lax.fori_loop(0, seq // qb, block, 0)


def _attention(qkv, rel_table, att_width):
    b, seq, _ = qkv.shape
    heads = att_width // ATT_HEAD_DIM
    hp = LANES // ATT_HEAD_DIM
    qb = min(ATT_QBLOCK, seq)
    pad = LEFT_CHUNKS * CHUNK
    kb = qb + pad
    ql = jnp.arange(qb)[:, None]
    kl = jnp.arange(kb)[None, :]
    span = qb + kb - 1
    rel_u = (qb - 1 + pad) - jnp.arange(span)
    u = rel_table[:, jnp.clip(rel_u, -MAX_REL, MAX_REL) + MAX_REL].astype(F32)
    u = jnp.concatenate([u, jnp.zeros((heads, 1), F32)], axis=1)
    skew = jnp.tile(u, (1, qb))[:, :qb * span].reshape(heads, qb, span)
    bias = skew[:, :, qb - 1:qb - 1 + kb]
    dchunk = ql // CHUNK + LEFT_CHUNKS - kl // CHUNK
    band = (dchunk >= 0) & (dchunk <= LEFT_CHUNKS)
    bias = jnp.where(band[None], bias, NEG)
    nblk = att_width // LANES
    return pl.pallas_call(
        functools.partial(_attn_kernel, seq=seq, pad=pad, qb=qb, kb=kb),
        grid=(b, heads // hp),
        in_specs=[pl.BlockSpec((1, seq, LANES), lambda i, j: (i, 0, j)),
                  pl.BlockSpec((1, seq, LANES), lambda i, j: (i, 0, nblk + j)),
                  pl.BlockSpec((1, seq, LANES), lambda i, j: (i, 0, 2 * nblk + j)),
                  pl.BlockSpec((hp, qb, kb), lambda i, j: (j, 0, 0))],
        out_specs=pl.BlockSpec((1, seq, LANES), lambda i, j: (i, 0, j)),
        out_shape=jax.ShapeDtypeStruct((b, seq, att_width), BF16),
        scratch_shapes=[pltpu.VMEM((seq + pad, LANES), BF16),
                        pltpu.VMEM((seq + pad, LANES), BF16)],
        compiler_params=_params("parallel", "parallel"),
        name="band_attention",
    )(qkv, qkv, qkv, bias)


def _lru_kernel(xr_ref, yg_ref, cw_ref, cb_ref, wg_ref, bg_ref, lam_ref, o_ref, *, seq):
    bw = xr_ref.shape[-1]
    x = xr_ref[0]
    row = lax.broadcasted_iota(I32, (seq, bw), 0)
    cw = cw_ref[...]
    xc = cb_ref[...] + cw[CONV_WIDTH - 1:CONV_WIDTH, :] * x
    for j in range(1, CONV_WIDTH):
        xs = jnp.where(row >= j, pltpu.roll(x, j, 0), 0.0)
        xc = xc + cw[CONV_WIDTH - 1 - j:CONV_WIDTH - j, :] * xs
    gates = jnp.dot(xc.astype(BF16), wg_ref[0], preferred_element_type=F32) + bg_ref[0]
    r = 1.0 / (1.0 + jnp.exp(-gates[:, :bw]))
    i = 1.0 / (1.0 + jnp.exp(-gates[:, bw:]))
    z = -lam_ref[...]
    softplus = jnp.maximum(z, 0.0) + jnp.log1p(jnp.exp(-jnp.abs(z)))
    a = jnp.exp(-LRU_C * r * softplus)
    b = jnp.sqrt(1.0 - a * a) * (i * xc)
    k = 1
    while k < seq:
        keep = row >= k
        b = jnp.where(keep, a * pltpu.roll(b, k, 0) + b, b)
        a = jnp.where(keep, a * pltpu.roll(a, k, 0), a)
        k *= 2
    g = yg_ref[0]
    gelu = 0.5 * g * (1.0 + jnp.tanh(0.7978845608028654 * (g + 0.044715 * (g * g * g))))
    o_ref[0] = (b * gelu).astype(o_ref.dtype)


def _lru(xy, conv_w, conv_b, w_ga, b_ga, w_gx, b_gx, lam, lru_width):
    b, seq, _ = xy.shape
    nb, bw, _ = w_ga.shape
    wg = jnp.concatenate([w_ga, w_gx], axis=-1).astype(BF16)
    bg = jnp.concatenate([b_ga, b_gx], axis=-1).reshape(nb, 1, 2 * bw)
    return pl.pallas_call(
        functools.partial(_lru_kernel, seq=seq),
        grid=(b, nb),
        in_specs=[pl.BlockSpec((1, seq, bw), lambda i, n: (i, 0, n)),
                  pl.BlockSpec((1, seq, bw), lambda i, n: (i, 0, nb + n)),
                  pl.BlockSpec((CONV_WIDTH, bw), lambda i, n: (0, n)),
                  pl.BlockSpec((1, bw), lambda i, n: (0, n)),
                  pl.BlockSpec((1, bw, 2 * bw), lambda i, n: (n, 0, 0)),
                  pl.BlockSpec((1, 1, 2 * bw), lambda i, n: (n, 0, 0)),
                  pl.BlockSpec((1, bw), lambda i, n: (0, n))],
        out_specs=pl.BlockSpec((1, seq, bw), lambda i, n: (i, 0, n)),
        out_shape=jax.ShapeDtypeStruct((b, seq, lru_width), BF16),
        compiler_params=_params("parallel", "parallel"),
        name="rg_lru",
    )(xy, xy, conv_w, conv_b.reshape(1, -1), wg, bg, lam.reshape(1, -1))


def _pack_bf16_pair(lo, hi):
    lo_b = pltpu.bitcast(lo.astype(BF16).astype(F32), U32) >> 16
    hi_b = pltpu.bitcast(hi.astype(BF16).astype(F32), U32) & jnp.uint32(0xFFFF0000)
    return hi_b | lo_b


def _unpack_bf16_pair(w):
    lo = pltpu.bitcast(w << 16, F32)
    hi = pltpu.bitcast(w & jnp.uint32(0xFFFF0000), F32)
    return lo, hi


def _mid_kernel(x_ref, att_ref, lru_ref, ck_ref, cv_ref, ga_ref, gl_ref, wo_ref, gc_ref, wcq_ref,
                wco_ref, gf_ref, wr_ref, br_ref,
                x2_ref, hp_ref, route_ref, gate_ref, cnt_ref, cnt_scr, *, n_experts):
    rows, d = x_ref.shape
    aw = att_ref.shape[-1]

    @pl.when((pl.program_id(0) == 0) & (pl.program_id(1) == 0))
    def _():
        cnt_scr[...] = jnp.zeros_like(cnt_scr)

    att_n = _rms(att_ref[...].astype(F32), ga_ref[...]).astype(BF16)
    lru_n = _rms(lru_ref[...].astype(F32), gl_ref[...]).astype(BF16)
    x1 = (x_ref[...]
          + jnp.dot(att_n, wo_ref[0:aw, :], preferred_element_type=F32)
          + jnp.dot(lru_n, wo_ref[aw:, :], preferred_element_type=F32))

    hq = _rms(x1, gc_ref[...]).astype(BF16)
    xw = wcq_ref.shape[-1]
    xhd = xw // X_HEADS
    cq = (jnp.dot(hq, wcq_ref[...], preferred_element_type=F32) * (xhd ** -0.5)).astype(BF16)
    ck = ck_ref[0]
    cv = cv_ref[0]
    heads = []
    for h in range(X_HEADS):
        sl = slice(h * xhd, (h + 1) * xhd)
        s = lax.dot_general(cq[:, sl], ck[:, sl], (((1,), (1,)), ((), ())),
                            preferred_element_type=F32)
        m = jnp.max(s, axis=-1, keepdims=True)
        p = jnp.exp(s - m)
        l = jnp.sum(p, axis=-1, keepdims=True)
        heads.append(jnp.dot(p.astype(BF16), cv[:, sl], preferred_element_type=F32) / l)
    o = jnp.concatenate(heads, axis=-1).astype(BF16)
    x2 = x1 + jnp.dot(o, wco_ref[...], preferred_element_type=F32)
    x2_ref[...] = x2

    hn = _rms(x2, gf_ref[...])
    hp_ref[...] = _pack_bf16_pair(hn[:, :d // 2], hn[:, d // 2:])

    logits = jnp.dot(hn.astype(BF16), wr_ref[...], preferred_element_type=F32) + br_ref[...]
    lane = lax.broadcasted_iota(I32, (rows, LANES), 1).astype(F32)
    work = logits
    sel_e, sel_v = [], []
    onehot = jnp.zeros((rows, LANES), F32)
    for _ in range(TOP_K):
        v = jnp.max(work, axis=-1, keepdims=True)
        e = jnp.min(jnp.where(work == v, lane, float(LANES)), axis=-1, keepdims=True)
        hit = lane == e
        onehot = jnp.where(hit, 1.0, onehot)
        work = jnp.where(hit, NEG * 2, work)
        sel_e.append(e)
        sel_v.append(v)
    ex = [jnp.exp(v - sel_v[0]) for v in sel_v]
    den = ex[0] + ex[1] + ex[2] + ex[3]

    ri = lax.broadcasted_iota(I32, (rows, rows), 0)
    ci = lax.broadcasted_iota(I32, (rows, rows), 1)
    tri = jnp.where(ci < ri, 1.0, 0.0).astype(BF16)
    before = jnp.dot(tri, onehot.astype(BF16), preferred_element_type=F32) + cnt_scr[...]
    route = jnp.zeros((rows, LANES), F32)
    gate = jnp.zeros((rows, LANES), F32)
    for k in range(TOP_K):
        rank = jnp.sum(jnp.where(lane == sel_e[k], before, 0.0), axis=-1, keepdims=True)
        route = jnp.where(lane == float(k), sel_e[k], route)
        route = jnp.where(lane == float(TOP_K + k), rank, route)
        gate = jnp.where(lane == float(k), ex[k] / den, gate)
    route_ref[...] = route.astype(I32)
    gate_ref[...] = gate
    cnt_scr[...] = cnt_scr[...] + jnp.sum(onehot, axis=0, keepdims=True)
    cnt_ref[...] = cnt_scr[...]


def _mid(x, att, lru, ckv, ga, gl, wo_bf, gc, wcq_bf, wco_bf, gf, wr_pad, br_pad, n_experts, rows):
    b, seq, d = x.shape
    aw, lw = att.shape[-1], lru.shape[-1]
    mem_len, xw2 = ckv.shape[1], ckv.shape[2]
    xw = xw2 // 2
    n = b * seq
    nt = seq // rows
    row_map = lambda i, t: (i * nt + t, 0)
    const = lambda i, t: (0, 0)
    res = lambda shape: pl.BlockSpec(shape, const, pipeline_mode=pl.Buffered(1))
    return pl.pallas_call(
        functools.partial(_mid_kernel, n_experts=n_experts),
        grid=(b, nt),
        in_specs=[pl.BlockSpec((rows, d), row_map),
                  pl.BlockSpec((rows, aw), row_map),
                  pl.BlockSpec((rows, lw), row_map),
                  pl.BlockSpec((1, mem_len, xw), lambda i, t: (i, 0, 0)),
                  pl.BlockSpec((1, mem_len, xw), lambda i, t: (i, 0, 1)),
                  res((1, aw)), res((1, lw)), res((aw + lw, d)), res((1, d)), res((d, xw)),
                  res((xw, d)), res((1, d)), res((d, LANES)), res((1, LANES))],
        out_specs=[pl.BlockSpec((rows, d), row_map),
                   pl.BlockSpec((rows, d // 2), row_map),
                   pl.BlockSpec((rows, LANES), row_map),
                   pl.BlockSpec((rows, LANES), row_map),
                   pl.BlockSpec((1, LANES), const)],
        out_shape=[jax.ShapeDtypeStruct((n, d), F32),
                   jax.ShapeDtypeStruct((n, d // 2), U32),
                   jax.ShapeDtypeStruct((n, LANES), I32),
                   jax.ShapeDtypeStruct((n, LANES), F32),
                   jax.ShapeDtypeStruct((1, LANES), F32)],
        scratch_shapes=[pltpu.VMEM((1, LANES), F32)],
        compiler_params=_params("arbitrary", "arbitrary"),
        name="mix_cross_router",
    )(x.reshape(n, d), att.reshape(n, aw), lru.reshape(n, lw), ckv, ckv,
      ga.reshape(1, aw), gl.reshape(1, lw), wo_bf, gc.reshape(1, d), wcq_bf, wco_bf,
      gf.reshape(1, d), wr_pad, br_pad)


def _dispatch_kernel(fill_ref, dest_ref, src_ref, xs_ref, buf, zbuf, lsem, ssem, zsem,
                     *, rows, n_fill, tail_chunk):
    i = pl.program_id(0)
    n = pl.num_programs(0)
    zrows = zbuf.shape[0]

    def load(t, slot):
        return pltpu.make_async_copy(src_ref.at[pl.ds(pl.multiple_of(t * rows, rows), rows), :],
                                     buf.at[slot], lsem.at[slot])

    def zero_copy(start, size):
        return pltpu.make_async_copy(zbuf.at[pl.ds(0, size), :], xs_ref.at[pl.ds(start, size), :],
                                     zsem)

    def fill_chunks(fn):
        for e in range(n_fill):
            start = fill_ref[e]
            length = fill_ref[n_fill + 1 + e]
            head = jnp.minimum((-start) & (SUBLANES - 1), length)
            for h in range(SUBLANES - 1):
                @pl.when(h < head)
                def _(row=start + h):
                    fn(zero_copy(row, 1))
            start = start + head
            rest = length - head
            size = zrows
            while size >= SUBLANES:
                part = rest & size
                @pl.when(part != 0)
                def _(start=start, size=size):
                    fn(zero_copy(pl.multiple_of(start, SUBLANES), size))
                start = start + part
                size //= 2
        tail_start = fill_ref[n_fill]
        n_tail = fill_ref[2 * n_fill + 1]

        def tail(c, carry):
            fn(zero_copy(pl.multiple_of(tail_start + c * tail_chunk, tail_chunk), tail_chunk))
            return carry

        lax.fori_loop(0, n_tail, tail, 0)

    @pl.when(i == 0)
    def _():
        load(0, 0).start()
        zbuf[...] = jnp.zeros_like(zbuf)
        fill_chunks(lambda c: c.start())
        fill_chunks(lambda c: c.wait())

    @pl.when(i + 1 < n)
    def _():
        load(i + 1, (i + 1) % 3).start()

    slot = i % 3
    par = i % 2
    load(i, slot).wait()

    def start_rows(g, c):
        base = pl.multiple_of(g * SUBLANES, SUBLANES)
        for j in range(SUBLANES):
            for k in range(TOP_K):
                dst = dest_ref[0, 0, (base + j) * TOP_K + k]
                pltpu.make_async_copy(buf.at[slot, pl.ds(base + j, 1), :],
                                      xs_ref.at[pl.ds(dst, 1), :], ssem.at[par]).start()
        return c

    lax.fori_loop(0, rows // SUBLANES, start_rows, 0)

    def wait_tile(p):
        for _ in range(TOP_K):
            pltpu.make_async_copy(buf.at[0], xs_ref.at[pl.ds(0, rows), :], ssem.at[p]).wait()

    @pl.when(i > 0)
    def _():
        wait_tile(1 - par)

    @pl.when(i == n - 1)
    def _():
        wait_tile(par)


def _dispatch(fill, dest, src, n_slots, rows, moe_rows):
    n_tok, w = src.shape
    nt = n_tok // rows
    n_fill = (fill.shape[0] - 2) // 2
    zrows = moe_rows // 2
    return pl.pallas_call(
        functools.partial(_dispatch_kernel, rows=rows, n_fill=n_fill, tail_chunk=zrows),
        grid_spec=pltpu.PrefetchScalarGridSpec(
            num_scalar_prefetch=1,
            grid=(nt,),
            in_specs=[pl.BlockSpec((1, 1, rows * TOP_K), lambda i, fill: (i, 0, 0),
                                   memory_space=pltpu.SMEM),
                      pl.BlockSpec(memory_space=pl.ANY)],
            out_specs=pl.BlockSpec(memory_space=pl.ANY),
            scratch_shapes=[pltpu.VMEM((3, rows, w), src.dtype),
                            pltpu.VMEM((zrows, w), src.dtype),
                            pltpu.SemaphoreType.DMA((3,)),
                            pltpu.SemaphoreType.DMA((2,)),
                            pltpu.SemaphoreType.DMA(())]),
        out_shape=jax.ShapeDtypeStruct((n_slots, w), src.dtype),
        compiler_params=_params("arbitrary"),
        name="dispatch_scatter",
    )(fill, dest.reshape(nt, 1, rows * TOP_K), src)


def _grouped_kernel(te_ref, meta_ref, x_ref, wa_ref, wb_ref, ba_ref, bb_ref, o_ref, wa_bf, wb_bf,
                    *, swiglu):
    i = pl.program_id(1)
    prev = te_ref[jnp.maximum(i - 1, 0)]
    new_expert = (i == 0) | (te_ref[i] != prev)

    @pl.when(new_expert)
    def _():
        wa_bf[...] = wa_ref[0].astype(BF16)
        wb_bf[...] = wb_ref[0].astype(BF16)

    @pl.when(i < meta_ref[0])
    def _():
        if swiglu:
            lo, hi = _unpack_bf16_pair(x_ref[...])
            xb = jnp.concatenate([lo.astype(BF16), hi.astype(BF16)], axis=-1)
        else:
            xb = x_ref[...]
        a = jnp.dot(xb, wa_bf[...], preferred_element_type=F32) + ba_ref[0]
        b = jnp.dot(xb, wb_bf[...], preferred_element_type=F32) + bb_ref[0]
        if swiglu:
            gate = jnp.minimum(a, SWIGLU_LIMIT)
            up = jnp.clip(b, -SWIGLU_LIMIT, SWIGLU_LIMIT)
            act = (up + 1.0) * gate * (1.0 / (1.0 + jnp.exp(-SWIGLU_ALPHA * gate)))
            o_ref[...] = act.astype(o_ref.dtype)
        else:
            o_ref[...] = _pack_bf16_pair(a, b)

    @pl.when(i >= meta_ref[0])
    def _():
        o_ref[...] = jnp.zeros_like(o_ref)


def _grouped_matmul(tile_expert, meta, x, w, bias, rows, cols, swiglu):
    n_slots = x.shape[0]
    n_tiles = n_slots // rows
    n_exp, kdim, n2 = w.shape
    half = n2 // 2
    nj = half // cols
    used = lambda i, meta: jnp.minimum(i, meta[0] - 1)
    out_dtype = BF16 if swiglu else U32
    return pl.pallas_call(
        functools.partial(_grouped_kernel, swiglu=swiglu),
        grid_spec=pltpu.PrefetchScalarGridSpec(
            num_scalar_prefetch=2,
            grid=(nj, n_tiles),
            in_specs=[pl.BlockSpec((rows, x.shape[1]), lambda j, i, te, meta: (used(i, meta), 0)),
                      pl.BlockSpec((1, kdim, cols), lambda j, i, te, meta: (te[i], 0, j)),
                      pl.BlockSpec((1, kdim, cols), lambda j, i, te, meta: (te[i], 0, nj + j)),
                      pl.BlockSpec((1, 1, cols), lambda j, i, te, meta: (te[i], 0, j)),
                      pl.BlockSpec((1, 1, cols), lambda j, i, te, meta: (te[i], 0, nj + j))],
            out_specs=pl.BlockSpec((rows, cols), lambda j, i, te, meta: (i, j)),
            scratch_shapes=[pltpu.VMEM((kdim, cols), BF16), pltpu.VMEM((kdim, cols), BF16)]),
        out_shape=jax.ShapeDtypeStruct((n_slots, half), out_dtype),
        compiler_params=_params("arbitrary", "arbitrary"),
        name="moe_gate_up" if swiglu else "moe_down",
    )(tile_expert, meta, x, w, w, bias.reshape(n_exp, 1, n2), bias.reshape(n_exp, 1, n2))


def _combine_kernel(dcur_ref, dnext_ref, y_ref, x2_ref, gate_ref, g_ref, o_ref, ybuf, sem, *, rows):
    i = pl.program_id(0)
    n = pl.num_programs(0)

    def gather_tile(d_ref, slot):
        def start(g, c):
            base = pl.multiple_of(g * SUBLANES, SUBLANES)
            for j in range(SUBLANES):
                for k in range(TOP_K):
                    src = d_ref[0, 0, (base + j) * TOP_K + k]
                    pltpu.make_async_copy(y_ref.at[pl.ds(src, 1), :],
                                          ybuf.at[slot, k, pl.ds(base + j, 1), :],
                                          sem.at[slot]).start()
            return c

        lax.fori_loop(0, rows // SUBLANES, start, 0)

    @pl.when(i == 0)
    def _():
        gather_tile(dcur_ref, 0)

    @pl.when(i + 1 < n)
    def _():
        gather_tile(dnext_ref, (i + 1) % 2)

    cur = i % 2
    for k in range(TOP_K):
        pltpu.make_async_copy(y_ref.at[pl.ds(0, rows), :], ybuf.at[cur, k], sem.at[cur]).wait()

    gate = gate_ref[...]
    half = ybuf.shape[-1]
    acc_lo = x2_ref[:, :half]
    acc_hi = x2_ref[:, half:]
    for k in range(TOP_K):
        lo, hi = _unpack_bf16_pair(ybuf[cur, k])
        gk = gate[:, k:k + 1]
        acc_lo = acc_lo + gk * lo
        acc_hi = acc_hi + gk * hi
    ms = (jnp.sum(acc_lo * acc_lo, axis=-1, keepdims=True)
          + jnp.sum(acc_hi * acc_hi, axis=-1, keepdims=True)) / (2 * half)
    inv = lax.rsqrt(ms + EPS)
    o_ref[:, :half] = acc_lo * inv * g_ref[:, :half]
    o_ref[:, half:] = acc_hi * inv * g_ref[:, half:]


def _combine(dest, y, x2, gate, g, rows):
    n, d = x2.shape
    nt = n // rows
    return pl.pallas_call(
        functools.partial(_combine_kernel, rows=rows),
        grid=(nt,),
        in_specs=[pl.BlockSpec((1, 1, rows * TOP_K), lambda i: (i, 0, 0), memory_space=pltpu.SMEM),
                  pl.BlockSpec((1, 1, rows * TOP_K), lambda i: (jnp.minimum(i + 1, nt - 1), 0, 0),
                               memory_space=pltpu.SMEM),
                  pl.BlockSpec(memory_space=pl.ANY),
                  pl.BlockSpec((rows, d), lambda i: (i, 0)),
                  pl.BlockSpec((rows, LANES), lambda i: (i, 0)),
                  pl.BlockSpec((1, d), lambda i: (0, 0))],
        out_specs=pl.BlockSpec((rows, d), lambda i: (i, 0)),
        out_shape=jax.ShapeDtypeStruct((n, d), F32),
        scratch_shapes=[pltpu.VMEM((2, TOP_K, rows, d // 2), U32), pltpu.SemaphoreType.DMA((2,))],
        compiler_params=_params("arbitrary"),
        name="combine_norm",
    )(dest.reshape(nt, 1, rows * TOP_K), dest.reshape(nt, 1, rows * TOP_K), y, x2, gate,
      g.reshape(1, d))


def kernel(x, mem, norm_mix_g, w_in, rel_table, conv_w, conv_b, w_ga, b_ga, w_gx, b_gx, lru_lambda, norm_att_out_g, norm_lru_out_g, w_out, norm_cross_g, norm_mem_g, w_cq, w_ckv, w_co, norm_ffn_g, w_router, b_router, w_gu, b_gu, w_down, b_down, norm_final_g):
    b, seq, d = x.shape
    depth = w_in.shape[0]
    att_width = d // 2
    lru_width = d - att_width
    n_experts = w_router.shape[-1]
    n_tok = b * seq
    mem_len = mem.shape[1]
    assert depth == 1 and n_experts <= LANES and seq % min(ATT_QBLOCK, seq) == 0

    moe_rows = min(MOE_ROWS, n_tok)
    n_tiles = (n_tok * TOP_K + n_experts * (moe_rows - 1)) // moe_rows
    n_slots = n_tiles * moe_rows

    for l in range(depth):
        qkv, xy = _norm_proj(x.reshape(n_tok, d), norm_mix_g[l], w_in[l].astype(BF16),
                             [(3 * att_width, BF16), (2 * lru_width, F32)], min(PROJ_ROWS, seq))
        att = _attention(qkv.reshape(b, seq, 3 * att_width), rel_table[l], att_width)
        lru = _lru(xy.reshape(b, seq, 2 * lru_width), conv_w[l], conv_b[l], w_ga[l], b_ga[l],
                   w_gx[l], b_gx[l], lru_lambda[l], lru_width)
        (ckv,) = _norm_proj(mem.reshape(b * mem_len, d), norm_mem_g[l], w_ckv[l].astype(BF16),
                            [(w_ckv.shape[-1], BF16)], min(PROJ_ROWS, mem_len))
        ckv = ckv.reshape(b, mem_len, -1)
        wr_pad = jnp.zeros((d, LANES), F32).at[:, :n_experts].set(w_router[l]).astype(BF16)
        br_pad = jnp.full((1, LANES), NEG, F32).at[0, :n_experts].set(b_router[l])
        x2, hpack, route, gate, counts = _mid(
            x, att, lru, ckv, norm_att_out_g[l], norm_lru_out_g[l], w_out[l].astype(BF16),
            norm_cross_g[l], w_cq[l].astype(BF16), w_co[l].astype(BF16), norm_ffn_g[l],
            wr_pad, br_pad, n_experts, min(MID_ROWS, seq))
        counts = counts[0, :n_experts].astype(I32)
        padded = (counts + moe_rows - 1) // moe_rows * moe_rows
        pad_end = jnp.cumsum(padded)
        pad_start = pad_end - padded
        expert_ids = jnp.arange(n_experts, dtype=I32)
        sel_e = route[:, :TOP_K]
        start_of = jnp.sum(jnp.where(sel_e[:, :, None] == expert_ids, pad_start, 0), axis=-1)
        dest = start_of + route[:, TOP_K:2 * TOP_K]
        total = pad_end[-1]
        meta = (total // moe_rows).astype(I32).reshape(1)
        tile_start = jnp.minimum(jnp.arange(n_tiles, dtype=I32) * moe_rows, total - moe_rows)
        tile_expert = jnp.sum(tile_start[:, None] >= pad_end[None, :], axis=1).astype(I32)
        fill = jnp.concatenate([pad_start + counts, total[None], padded - counts,
                                ((n_slots - total) // (moe_rows // 2))[None]]).astype(I32)
        xs = _dispatch(fill, dest, hpack, n_slots, min(DISPATCH_ROWS, n_tok), moe_rows)
        cols = min(MOE_COLS, w_down.shape[2] // 2)
        act = _grouped_matmul(tile_expert, meta, xs, w_gu[l], b_gu[l], moe_rows, cols, True)
        y = _grouped_matmul(tile_expert, meta, act, w_down[l], b_down[l], moe_rows,
                            min(MOE_COLS, d // 2), False)
        x = _combine(dest, y, x2, gate, norm_final_g, min(COMBINE_ROWS, n_tok)).reshape(b, seq, d)
    return x
```

```python
import functools

import jax
import jax.numpy as jnp
from jax import lax
from jax.experimental import pallas as pl
from jax.experimental.pallas import tpu as pltpu

F32 = jnp.float32
BF16 = jnp.bfloat16
U32 = jnp.uint32
I32 = jnp.int32

EPS = 1e-6
CHUNK = 64
LEFT_CHUNKS = 8
ATT_HEAD_DIM = 64
MAX_REL = 2 * CHUNK
CONV_WIDTH = 4
LRU_C = 8.0
X_HEADS = 4
TOP_K = 4
SWIGLU_ALPHA = 1.702
SWIGLU_LIMIT = 7.0
NEG = -1e30

LANES = 128
SUBLANES = 8
MXU_COLS = 256
VMEM_LIMIT = 56 * 1024 * 1024

PROJ_ROWS = 256
ATT_QBLOCK = 256
MID_ROWS = 256
MOE_ROWS = 512
MOE_COLS = 1024
DISPATCH_ROWS = 256
COMBINE_ROWS = 256


def _rms(x, g):
    ms = jnp.mean(x * x, axis=-1, keepdims=True)
    return x * lax.rsqrt(ms + EPS) * g


def _params(*sem):
    return pltpu.CompilerParams(dimension_semantics=sem, vmem_limit_bytes=VMEM_LIMIT)


def _norm_proj_kernel(x_ref, g_ref, w_ref, *out_refs, col_chunk):
    h = _rms(x_ref[...], g_ref[...]).astype(BF16)
    c0 = 0
    for o_ref in out_refs:
        n = o_ref.shape[-1]
        for s in range(0, n, col_chunk):
            e = min(s + col_chunk, n)
            o_ref[:, s:e] = jnp.dot(h, w_ref[:, c0 + s:c0 + e],
                                    preferred_element_type=F32).astype(o_ref.dtype)
        c0 += n


def _norm_proj(x2d, g, w_bf, outs, rows):
    m, d = x2d.shape
    n_total = w_bf.shape[1]
    assert sum(n for n, _ in outs) == n_total and m % rows == 0
    return pl.pallas_call(
        functools.partial(_norm_proj_kernel, col_chunk=512),
        grid=(m // rows,),
        in_specs=[pl.BlockSpec((rows, d), lambda i: (i, 0)),
                  pl.BlockSpec((1, d), lambda i: (0, 0)),
                  pl.BlockSpec((d, n_total), lambda i: (0, 0), pipeline_mode=pl.Buffered(1))],
        out_specs=[pl.BlockSpec((rows, n), lambda i: (i, 0)) for n, _ in outs],
        out_shape=[jax.ShapeDtypeStruct((m, n), dt) for n, dt in outs],
        compiler_params=_params("parallel"),
        name="norm_proj",
    )(x2d, g.reshape(1, d), w_bf)


def _attn_kernel(q_ref, k_ref, v_ref, bias_ref, o_ref, kpad, vpad, *, seq, pad, qb, kb):
    hd = ATT_HEAD_DIM
    kpad[0:pad, :] = jnp.zeros((pad, LANES), BF16)
    vpad[0:pad, :] = jnp.zeros((pad, LANES), BF16)
    kpad[pad:pad + seq, :] = k_ref[0]
    vpad[pad:pad + seq, :] = v_ref[0]
    scale = hd ** -0.5
    kcol = lax.broadcasted_iota(I32, (qb, kb), 1)

    def block(ib, carry):
        s0 = pl.multiple_of(ib * qb, qb)
        q = q_ref[0, pl.ds(s0, qb), :]
        kblk = kpad[pl.ds(s0, kb), :]
        vblk = vpad[pl.ds(s0, kb), :]
        before_start = kcol < (pad - s0)
        outs = []
        for hh in range(LANES // hd):
            sl = slice(hh * hd, (hh + 1) * hd)
            s = lax.dot_general(q[:, sl], kblk[:, sl], (((1,), (1,)), ((), ())),
                                preferred_element_type=F32)
            s = s * scale + bias_ref[hh]
            s = jnp.where(before_start, NEG, s)
            m = jnp.max(s, axis=-1, keepdims=True)
            p = jnp.exp(s - m)
            l = jnp.sum(p, axis=-1, keepdims=True)
            o = jnp.dot(p.astype(BF16), vblk[:, sl], preferred_element_type=F32)
            outs.append(o / l)
        o_ref[0, pl.ds(s0, qb), :] = jnp.concatenate(outs, axis=-1).astype(o_ref.dtype)
        return carry

    lax.fori_loop(0, seq // qb, block, 0)


def _attention(qkv, rel_table, att_width):
    b, seq, _ = qkv.shape
    heads = att_width // ATT_HEAD_DIM
    hp = LANES // ATT_HEAD_DIM
    qb = min(ATT_QBLOCK, seq)
    pad = LEFT_CHUNKS * CHUNK
    kb = qb + pad
    ql = jnp.arange(qb)[:, None]
    kl = jnp.arange(kb)[None, :]
    span = qb + kb - 1
    rel_u = (qb - 1 + pad) - jnp.arange(span)
    u = rel_table[:, jnp.clip(rel_u, -MAX_REL, MAX_REL) + MAX_REL].astype(F32)
    u = jnp.concatenate([u, jnp.zeros((heads, 1), F32)], axis=1)
    skew = jnp.tile(u, (1, qb))[:, :qb * span].reshape(heads, qb, span)
    bias = skew[:, :, qb - 1:qb - 1 + kb]
    dchunk = ql // CHUNK + LEFT_CHUNKS - kl // CHUNK
    band = (dchunk >= 0) & (dchunk <= LEFT_CHUNKS)
    bias = jnp.where(band[None], bias, NEG)
    nblk = att_width // LANES
    return pl.pallas_call(
        functools.partial(_attn_kernel, seq=seq, pad=pad, qb=qb, kb=kb),
        grid=(b, heads // hp),
        in_specs=[pl.BlockSpec((1, seq, LANES), lambda i, j: (i, 0, j)),
                  pl.BlockSpec((1, seq, LANES), lambda i, j: (i, 0, nblk + j)),
                  pl.BlockSpec((1, seq, LANES), lambda i, j: (i, 0, 2 * nblk + j)),
                  pl.BlockSpec((hp, qb, kb), lambda i, j: (j, 0, 0))],
        out_specs=pl.BlockSpec((1, seq, LANES), lambda i, j: (i, 0, j)),
        out_shape=jax.ShapeDtypeStruct((b, seq, att_width), BF16),
        scratch_shapes=[pltpu.VMEM((seq + pad, LANES), BF16),
                        pltpu.VMEM((seq + pad, LANES), BF16)],
        compiler_params=_params("parallel", "parallel"),
        name="band_attention",
    )(qkv, qkv, qkv, bias)


def _lru_kernel(xr_ref, yg_ref, cw_ref, cb_ref, wg_ref, bg_ref, lam_ref, o_ref, *, seq):
    bw = xr_ref.shape[-1]
    x = xr_ref[0]
    row = lax.broadcasted_iota(I32, (seq, bw), 0)
    cw = cw_ref[...]
    xc = cb_ref[...] + cw[CONV_WIDTH - 1:CONV_WIDTH, :] * x
    for j in range(1, CONV_WIDTH):
        xs = jnp.where(row >= j, pltpu.roll(x, j, 0), 0.0)
        xc = xc + cw[CONV_WIDTH - 1 - j:CONV_WIDTH - j, :] * xs
    gates = jnp.dot(xc.astype(BF16), wg_ref[0], preferred_element_type=F32) + bg_ref[0]
    r = 1.0 / (1.0 + jnp.exp(-gates[:, :bw]))
    i = 1.0 / (1.0 + jnp.exp(-gates[:, bw:]))
    z = -lam_ref[...]
    softplus = jnp.maximum(z, 0.0) + jnp.log1p(jnp.exp(-jnp.abs(z)))
    a = jnp.exp(-LRU_C * r * softplus)
    b = jnp.sqrt(1.0 - a * a) * (i * xc)
    k = 1
    while k < seq:
        keep = row >= k
        b = jnp.where(keep, a * pltpu.roll(b, k, 0) + b, b)
        a = jnp.where(keep, a * pltpu.roll(a, k, 0), a)
        k *= 2
    g = yg_ref[0]
    gelu = 0.5 * g * (1.0 + jnp.tanh(0.7978845608028654 * (g + 0.044715 * (g * g * g))))
    o_ref[0] = (b * gelu).astype(o_ref.dtype)


def _lru(xy, conv_w, conv_b, w_ga, b_ga, w_gx, b_gx, lam, lru_width):
    b, seq, _ = xy.shape
    nb, bw, _ = w_ga.shape
    wg = jnp.concatenate([w_ga, w_gx], axis=-1).astype(BF16)
    bg = jnp.concatenate([b_ga, b_gx], axis=-1).reshape(nb, 1, 2 * bw)
    return pl.pallas_call(
        functools.partial(_lru_kernel, seq=seq),
        grid=(b, nb),
        in_specs=[pl.BlockSpec((1, seq, bw), lambda i, n: (i, 0, n)),
                  pl.BlockSpec((1, seq, bw), lambda i, n: (i, 0, nb + n)),
                  pl.BlockSpec((CONV_WIDTH, bw), lambda i, n: (0, n)),
                  pl.BlockSpec((1, bw), lambda i, n: (0, n)),
                  pl.BlockSpec((1, bw, 2 * bw), lambda i, n: (n, 0, 0)),
                  pl.BlockSpec((1, 1, 2 * bw), lambda i, n: (n, 0, 0)),
                  pl.BlockSpec((1, bw), lambda i, n: (0, n))],
        out_specs=pl.BlockSpec((1, seq, bw), lambda i, n: (i, 0, n)),
        out_shape=jax.ShapeDtypeStruct((b, seq, lru_width), BF16),
        compiler_params=_params("parallel", "parallel"),
        name="rg_lru",
    )(xy, xy, conv_w, conv_b.reshape(1, -1), wg, bg, lam.reshape(1, -1))


def _pack_bf16_pair(lo, hi):
    lo_b = pltpu.bitcast(lo.astype(BF16).astype(F32), U32) >> 16
    hi_b = pltpu.bitcast(hi.astype(BF16).astype(F32), U32) & jnp.uint32(0xFFFF0000)
    return hi_b | lo_b


def _unpack_bf16_pair(w):
    lo = pltpu.bitcast(w << 16, F32)
    hi = pltpu.bitcast(w & jnp.uint32(0xFFFF0000), F32)
    return lo, hi


def _mid_kernel(x_ref, att_ref, lru_ref, ck_ref, cv_ref, ga_ref, gl_ref, wo_ref, gc_ref, wcq_ref,
                wco_ref, gf_ref, wr_ref, br_ref,
                x2_ref, hp_ref, route_ref, gate_ref, cnt_ref, cnt_scr, *, n_experts):
    rows, d = x_ref.shape
    aw = att_ref.shape[-1]

    @pl.when((pl.program_id(0) == 0) & (pl.program_id(1) == 0))
    def _():
        cnt_scr[...] = jnp.zeros_like(cnt_scr)

    att_n = _rms(att_ref[...].astype(F32), ga_ref[...]).astype(BF16)
    lru_n = _rms(lru_ref[...].astype(F32), gl_ref[...]).astype(BF16)
    x1 = (x_ref[...]
          + jnp.dot(att_n, wo_ref[0:aw, :], preferred_element_type=F32)
          + jnp.dot(lru_n, wo_ref[aw:, :], preferred_element_type=F32))

    hq = _rms(x1, gc_ref[...]).astype(BF16)
    xw = wcq_ref.shape[-1]
    xhd = xw // X_HEADS
    cq = (jnp.dot(hq, wcq_ref[...], preferred_element_type=F32) * (xhd ** -0.5)).astype(BF16)
    ck = ck_ref[0]
    cv = cv_ref[0]
    heads = []
    for h in range(X_HEADS):
        sl = slice(h * xhd, (h + 1) * xhd)
        s = lax.dot_general(cq[:, sl], ck[:, sl], (((1,), (1,)), ((), ())),
                            preferred_element_type=F32)
        m = jnp.max(s, axis=-1, keepdims=True)
        p = jnp.exp(s - m)
        l = jnp.sum(p, axis=-1, keepdims=True)
        heads.append(jnp.dot(p.astype(BF16), cv[:, sl], preferred_element_type=F32) / l)
    o = jnp.concatenate(heads, axis=-1).astype(BF16)
    x2 = x1 + jnp.dot(o, wco_ref[...], preferred_element_type=F32)
    x2_ref[...] = x2

    hn = _rms(x2, gf_ref[...])
    hp_ref[...] = _pack_bf16_pair(hn[:, :d // 2], hn[:, d // 2:])

    logits = jnp.dot(hn.astype(BF16), wr_ref[...], preferred_element_type=F32) + br_ref[...]
    lane = lax.broadcasted_iota(I32, (rows, LANES), 1).astype(F32)
    work = logits
    sel_e, sel_v = [], []
    onehot = jnp.zeros((rows, LANES), F32)
    for _ in range(TOP_K):
        v = jnp.max(work, axis=-1, keepdims=True)
        e = jnp.min(jnp.where(work == v, lane, float(LANES)), axis=-1, keepdims=True)
        hit = lane == e
        onehot = jnp.where(hit, 1.0, onehot)
        work = jnp.where(hit, NEG * 2, work)
        sel_e.append(e)
        sel_v.append(v)
    ex = [jnp.exp(v - sel_v[0]) for v in sel_v]
    den = ex[0] + ex[1] + ex[2] + ex[3]

    ri = lax.broadcasted_iota(I32, (rows, rows), 0)
    ci = lax.broadcasted_iota(I32, (rows, rows), 1)
    tri = jnp.where(ci < ri, 1.0, 0.0).astype(BF16)
    before = jnp.dot(tri, onehot.astype(BF16), preferred_element_type=F32) + cnt_scr[...]
    route = jnp.zeros((rows, LANES), F32)
    gate = jnp.zeros((rows, LANES), F32)
    for k in range(TOP_K):
        rank = jnp.sum(jnp.where(lane == sel_e[k], before, 0.0), axis=-1, keepdims=True)
        route = jnp.where(lane == float(k), sel_e[k], route)
        route = jnp.where(lane == float(TOP_K + k), rank, route)
        gate = jnp.where(lane == float(k), ex[k] / den, gate)
    route_ref[...] = route.astype(I32)
    gate_ref[...] = gate
    cnt_scr[...] = cnt_scr[...] + jnp.sum(onehot, axis=0, keepdims=True)
    cnt_ref[...] = cnt_scr[...]


def _mid(x, att, lru, ckv, ga, gl, wo_bf, gc, wcq_bf, wco_bf, gf, wr_pad, br_pad, n_experts, rows):
    b, seq, d = x.shape
    aw, lw = att.shape[-1], lru.shape[-1]
    mem_len, xw2 = ckv.shape[1], ckv.shape[2]
    xw = xw2 // 2
    n = b * seq
    nt = seq // rows
    row_map = lambda i, t: (i * nt + t, 0)
    const = lambda i, t: (0, 0)
    res = lambda shape: pl.BlockSpec(shape, const, pipeline_mode=pl.Buffered(1))
    return pl.pallas_call(
        functools.partial(_mid_kernel, n_experts=n_experts),
        grid=(b, nt),
        in_specs=[pl.BlockSpec((rows, d), row_map),
                  pl.BlockSpec((rows, aw), row_map),
                  pl.BlockSpec((rows, lw), row_map),
                  pl.BlockSpec((1, mem_len, xw), lambda i, t: (i, 0, 0)),
                  pl.BlockSpec((1, mem_len, xw), lambda i, t: (i, 0, 1)),
                  res((1, aw)), res((1, lw)), res((aw + lw, d)), res((1, d)), res((d, xw)),
                  res((xw, d)), res((1, d)), res((d, LANES)), res((1, LANES))],
        out_specs=[pl.BlockSpec((rows, d), row_map),
                   pl.BlockSpec((rows, d // 2), row_map),
                   pl.BlockSpec((rows, LANES), row_map),
                   pl.BlockSpec((rows, LANES), row_map),
                   pl.BlockSpec((1, LANES), const)],
        out_shape=[jax.ShapeDtypeStruct((n, d), F32),
                   jax.ShapeDtypeStruct((n, d // 2), U32),
                   jax.ShapeDtypeStruct((n, LANES), I32),
                   jax.ShapeDtypeStruct((n, LANES), F32),
                   jax.ShapeDtypeStruct((1, LANES), F32)],
        scratch_shapes=[pltpu.VMEM((1, LANES), F32)],
        compiler_params=_params("arbitrary", "arbitrary"),
        name="mix_cross_router",
    )(x.reshape(n, d), att.reshape(n, aw), lru.reshape(n, lw), ckv, ckv,
      ga.reshape(1, aw), gl.reshape(1, lw), wo_bf, gc.reshape(1, d), wcq_bf, wco_bf,
      gf.reshape(1, d), wr_pad, br_pad)


def _dispatch_kernel(fill_ref, dest_ref, src_ref, xs_ref, buf, zbuf, lsem, ssem, zsem,
                     *, rows, n_fill, tail_chunk):
    i = pl.program_id(0)
    n = pl.num_programs(0)
    zrows = zbuf.shape[0]

    def load(t, slot):
        return pltpu.make_async_copy(src_ref.at[pl.ds(pl.multiple_of(t * rows, rows), rows), :],
                                     buf.at[slot], lsem.at[slot])

    def zero_copy(start, size):
        return pltpu.make_async_copy(zbuf.at[pl.ds(0, size), :], xs_ref.at[pl.ds(start, size), :],
                                     zsem)

    def fill_chunks(fn):
        for e in range(n_fill):
            start = fill_ref[e]
            length = fill_ref[n_fill + 1 + e]
            head = jnp.minimum((-start) & (SUBLANES - 1), length)
            for h in range(SUBLANES - 1):
                @pl.when(h < head)
                def _(row=start + h):
                    fn(zero_copy(row, 1))
            start = start + head
            rest = length - head
            size = zrows
            while size >= SUBLANES:
                part = rest & size
                @pl.when(part != 0)
                def _(start=start, size=size):
                    fn(zero_copy(pl.multiple_of(start, SUBLANES), size))
                start = start + part
                size //= 2
        tail_start = fill_ref[n_fill]
        n_tail = fill_ref[2 * n_fill + 1]

        def tail(c, carry):
            fn(zero_copy(pl.multiple_of(tail_start + c * tail_chunk, tail_chunk), tail_chunk))
            return carry

        lax.fori_loop(0, n_tail, tail, 0)

    @pl.when(i == 0)
    def _():
        load(0, 0).start()
        zbuf[...] = jnp.zeros_like(zbuf)
        fill_chunks(lambda c: c.start())
        fill_chunks(lambda c: c.wait())

    @pl.when(i + 1 < n)
    def _():
        load(i + 1, (i + 1) % 3).start()

    slot = i % 3
    par = i % 2
    load(i, slot).wait()

    def start_rows(g, c):
        base = pl.multiple_of(g * SUBLANES, SUBLANES)
        for j in range(SUBLANES):
            for k in range(TOP_K):
                dst = dest_ref[0, 0, (base + j) * TOP_K + k]
                pltpu.make_async_copy(buf.at[slot, pl.ds(base + j, 1), :],
                                      xs_ref.at[pl.ds(dst, 1), :], ssem.at[par]).start()
        return c

    lax.fori_loop(0, rows // SUBLANES, start_rows, 0)

    def wait_tile(p):
        for _ in range(TOP_K):
            pltpu.make_async_copy(buf.at[0], xs_ref.at[pl.ds(0, rows), :], ssem.at[p]).wait()

    @pl.when(i > 0)
    def _():
        wait_tile(1 - par)

    @pl.when(i == n - 1)
    def _():
        wait_tile(par)


def _dispatch(fill, dest, src, n_slots, rows, moe_rows):
    n_tok, w = src.shape
    nt = n_tok // rows
    n_fill = (fill.shape[0] - 2) // 2
    zrows = moe_rows // 2
    return pl.pallas_call(
        functools.partial(_dispatch_kernel, rows=rows, n_fill=n_fill, tail_chunk=zrows),
        grid_spec=pltpu.PrefetchScalarGridSpec(
            num_scalar_prefetch=1,
            grid=(nt,),
            in_specs=[pl.BlockSpec((1, 1, rows * TOP_K), lambda i, fill: (i, 0, 0),
                                   memory_space=pltpu.SMEM),
                      pl.BlockSpec(memory_space=pl.ANY)],
            out_specs=pl.BlockSpec(memory_space=pl.ANY),
            scratch_shapes=[pltpu.VMEM((3, rows, w), src.dtype),
                            pltpu.VMEM((zrows, w), src.dtype),
                            pltpu.SemaphoreType.DMA((3,)),
                            pltpu.SemaphoreType.DMA((2,)),
                            pltpu.SemaphoreType.DMA(())]),
        out_shape=jax.ShapeDtypeStruct((n_slots, w), src.dtype),
        compiler_params=_params("arbitrary"),
        name="dispatch_scatter",
    )(fill, dest.reshape(nt, 1, rows * TOP_K), src)


def _grouped_kernel(te_ref, meta_ref, x_ref, wa_ref, wb_ref, ba_ref, bb_ref, o_ref, wa_bf, wb_bf,
                    *, swiglu):
    i = pl.program_id(1)
    prev = te_ref[jnp.maximum(i - 1, 0)]
    new_expert = (i == 0) | (te_ref[i] != prev)

    @pl.when(new_expert)
    def _():
        wa_bf[...] = wa_ref[0].astype(BF16)
        wb_bf[...] = wb_ref[0].astype(BF16)

    @pl.when(i < meta_ref[0])
    def _():
        if swiglu:
            lo, hi = _unpack_bf16_pair(x_ref[...])
            xb = jnp.concatenate([lo.astype(BF16), hi.astype(BF16)], axis=-1)
        else:
            xb = x_ref[...]
        for c0 in range(0, o_ref.shape[-1], MXU_COLS):
            cs = slice(c0, c0 + MXU_COLS)
            a = jnp.dot(xb, wa_bf[:, cs], preferred_element_type=F32) + ba_ref[0, :, cs]
            b = jnp.dot(xb, wb_bf[:, cs], preferred_element_type=F32) + bb_ref[0, :, cs]
            if swiglu:
                gate = jnp.minimum(a, SWIGLU_LIMIT)
                up = jnp.clip(b, -SWIGLU_LIMIT, SWIGLU_LIMIT)
                act = (up + 1.0) * gate * (1.0 / (1.0 + jnp.exp(-SWIGLU_ALPHA * gate)))
                o_ref[:, cs] = act.astype(o_ref.dtype)
            else:
                o_ref[:, cs] = _pack_bf16_pair(a, b)

    @pl.when(i >= meta_ref[0])
    def _():
        o_ref[...] = jnp.zeros_like(o_ref)


def _grouped_matmul(tile_expert, meta, x, w, bias, rows, cols, swiglu):
    n_slots = x.shape[0]
    n_tiles = n_slots // rows
    n_exp, kdim, n2 = w.shape
    half = n2 // 2
    nj = half // cols
    used = lambda i, meta: jnp.minimum(i, meta[0] - 1)
    out_dtype = BF16 if swiglu else U32
    return pl.pallas_call(
        functools.partial(_grouped_kernel, swiglu=swiglu),
        grid_spec=pltpu.PrefetchScalarGridSpec(
            num_scalar_prefetch=2,
            grid=(nj, n_tiles),
            in_specs=[pl.BlockSpec((rows, x.shape[1]), lambda j, i, te, meta: (used(i, meta), 0)),
                      pl.BlockSpec((1, kdim, cols), lambda j, i, te, meta: (te[i], 0, j)),
                      pl.BlockSpec((1, kdim, cols), lambda j, i, te, meta: (te[i], 0, nj + j)),
                      pl.BlockSpec((1, 1, cols), lambda j, i, te, meta: (te[i], 0, j)),
                      pl.BlockSpec((1, 1, cols), lambda j, i, te, meta: (te[i], 0, nj + j))],
            out_specs=pl.BlockSpec((rows, cols), lambda j, i, te, meta: (i, j)),
            scratch_shapes=[pltpu.VMEM((kdim, cols), BF16), pltpu.VMEM((kdim, cols), BF16)]),
        out_shape=jax.ShapeDtypeStruct((n_slots, half), out_dtype),
        compiler_params=_params("arbitrary", "arbitrary"),
        name="moe_gate_up" if swiglu else "moe_down",
    )(tile_expert, meta, x, w, w, bias.reshape(n_exp, 1, n2), bias.reshape(n_exp, 1, n2))


def _combine_kernel(dcur_ref, dnext_ref, y_ref, x2_ref, gate_ref, g_ref, o_ref, ybuf, sem, *, rows):
    i = pl.program_id(0)
    n = pl.num_programs(0)

    def gather_tile(d_ref, slot):
        def start(g, c):
            base = pl.multiple_of(g * SUBLANES, SUBLANES)
            for j in range(SUBLANES):
                for k in range(TOP_K):
                    src = d_ref[0, 0, (base + j) * TOP_K + k]
                    pltpu.make_async_copy(y_ref.at[pl.ds(src, 1), :],
                                          ybuf.at[slot, k, pl.ds(base + j, 1), :],
                                          sem.at[slot]).start()
            return c

        lax.fori_loop(0, rows // SUBLANES, start, 0)

    @pl.when(i == 0)
    def _():
        gather_tile(dcur_ref, 0)

    @pl.when(i + 1 < n)
    def _():
        gather_tile(dnext_ref, (i + 1) % 2)

    cur = i % 2
    for k in range(TOP_K):
        pltpu.make_async_copy(y_ref.at[pl.ds(0, rows), :], ybuf.at[cur, k], sem.at[cur]).wait()

    gate = gate_ref[...]
    half = ybuf.shape[-1]
    acc_lo = x2_ref[:, :half]
    acc_hi = x2_ref[:, half:]
    for k in range(TOP_K):
        lo, hi = _unpack_bf16_pair(ybuf[cur, k])
        gk = gate[:, k:k + 1]
        acc_lo = acc_lo + gk * lo
        acc_hi = acc_hi + gk * hi
    ms = (jnp.sum(acc_lo * acc_lo, axis=-1, keepdims=True)
          + jnp.sum(acc_hi * acc_hi, axis=-1, keepdims=True)) / (2 * half)
    inv = lax.rsqrt(ms + EPS)
    o_ref[:, :half] = acc_lo * inv * g_ref[:, :half]
    o_ref[:, half:] = acc_hi * inv * g_ref[:, half:]


def _combine(dest, y, x2, gate, g, rows):
    n, d = x2.shape
    nt = n // rows
    return pl.pallas_call(
        functools.partial(_combine_kernel, rows=rows),
        grid=(nt,),
        in_specs=[pl.BlockSpec((1, 1, rows * TOP_K), lambda i: (i, 0, 0), memory_space=pltpu.SMEM),
                  pl.BlockSpec((1, 1, rows * TOP_K), lambda i: (jnp.minimum(i + 1, nt - 1), 0, 0),
                               memory_space=pltpu.SMEM),
                  pl.BlockSpec(memory_space=pl.ANY),
                  pl.BlockSpec((rows, d), lambda i: (i, 0)),
                  pl.BlockSpec((rows, LANES), lambda i: (i, 0)),
                  pl.BlockSpec((1, d), lambda i: (0, 0))],
        out_specs=pl.BlockSpec((rows, d), lambda i: (i, 0)),
        out_shape=jax.ShapeDtypeStruct((n, d), F32),
        scratch_shapes=[pltpu.VMEM((2, TOP_K, rows, d // 2), U32), pltpu.SemaphoreType.DMA((2,))],
        compiler_params=_params("arbitrary"),
        name="combine_norm",
    )(dest.reshape(nt, 1, rows * TOP_K), dest.reshape(nt, 1, rows * TOP_K), y, x2, gate,
      g.reshape(1, d))


def kernel(x, mem, norm_mix_g, w_in, rel_table, conv_w, conv_b, w_ga, b_ga, w_gx, b_gx, lru_lambda, norm_att_out_g, norm_lru_out_g, w_out, norm_cross_g, norm_mem_g, w_cq, w_ckv, w_co, norm_ffn_g, w_router, b_router, w_gu, b_gu, w_down, b_down, norm_final_g):
    b, seq, d = x.shape
    depth = w_in.shape[0]
    att_width = d // 2
    lru_width = d - att_width
    n_experts = w_router.shape[-1]
    n_tok = b * seq
    mem_len = mem.shape[1]
    assert depth == 1 and n_experts <= LANES and seq % min(ATT_QBLOCK, seq) == 0

    moe_rows = min(MOE_ROWS, n_tok)
    n_tiles = (n_tok * TOP_K + n_experts * (moe_rows - 1)) // moe_rows
    n_slots = n_tiles * moe_rows

    for l in range(depth):
        qkv, xy = _norm_proj(x.reshape(n_tok, d), norm_mix_g[l], w_in[l].astype(BF16),
                             [(3 * att_width, BF16), (2 * lru_width, F32)], min(PROJ_ROWS, seq))
        att = _attention(qkv.reshape(b, seq, 3 * att_width), rel_table[l], att_width)
        lru = _lru(xy.reshape(b, seq, 2 * lru_width), conv_w[l], conv_b[l], w_ga[l], b_ga[l],
                   w_gx[l], b_gx[l], lru_lambda[l], lru_width)
        (ckv,) = _norm_proj(mem.reshape(b * mem_len, d), norm_mem_g[l], w_ckv[l].astype(BF16),
                            [(w_ckv.shape[-1], BF16)], min(PROJ_ROWS, mem_len))
        ckv = ckv.reshape(b, mem_len, -1)
        wr_pad = jnp.zeros((d, LANES), F32).at[:, :n_experts].set(w_router[l]).astype(BF16)
        br_pad = jnp.full((1, LANES), NEG, F32).at[0, :n_experts].set(b_router[l])
        x2, hpack, route, gate, counts = _mid(
            x, att, lru, ckv, norm_att_out_g[l], norm_lru_out_g[l], w_out[l].astype(BF16),
            norm_cross_g[l], w_cq[l].astype(BF16), w_co[l].astype(BF16), norm_ffn_g[l],
            wr_pad, br_pad, n_experts, min(MID_ROWS, seq))
        counts = counts[0, :n_experts].astype(I32)
        padded = (counts + moe_rows - 1) // moe_rows * moe_rows
        pad_end = jnp.cumsum(padded)
        pad_start = pad_end - padded
        expert_ids = jnp.arange(n_experts, dtype=I32)
        sel_e = route[:, :TOP_K]
        start_of = jnp.sum(jnp.where(sel_e[:, :, None] == expert_ids, pad_start, 0), axis=-1)
        dest = start_of + route[:, TOP_K:2 * TOP_K]
        total = pad_end[-1]
        meta = (total // moe_rows).astype(I32).reshape(1)
        tile_start = jnp.minimum(jnp.arange(n_tiles, dtype=I32) * moe_rows, total - moe_rows)
        tile_expert = jnp.sum(tile_start[:, None] >= pad_end[None, :], axis=1).astype(I32)
        fill = jnp.concatenate([pad_start + counts, total[None], padded - counts,
                                ((n_slots - total) // (moe_rows // 2))[None]]).astype(I32)
        xs = _dispatch(fill, dest, hpack, n_slots, min(DISPATCH_ROWS, n_tok), moe_rows)
        cols = min(MOE_COLS, w_down.shape[2] // 2)
        act = _grouped_matmul(tile_expert, meta, xs, w_gu[l], b_gu[l], moe_rows, cols, True)
        y = _grouped_matmul(tile_expert, meta, act, w_down[l], b_down[l], moe_rows,
                            min(MOE_COLS, d // 2), False)
        x = _combine(dest, y, x2, gate, norm_final_g, min(COMBINE_ROWS, n_tok)).reshape(b, seq, d)
    return x
```

```python
import functools

import jax
import jax.numpy as jnp
from jax import lax
from jax.experimental import pallas as pl
from jax.experimental.pallas import tpu as pltpu

F32 = jnp.float32
BF16 = jnp.bfloat16
U32 = jnp.uint32
I32 = jnp.int32

EPS = 1e-6
CHUNK = 64
LEFT_CHUNKS = 8
ATT_HEAD_DIM = 64
MAX_REL = 2 * CHUNK
CONV_WIDTH = 4
LRU_C = 8.0
X_HEADS = 4
TOP_K = 4
SWIGLU_ALPHA = 1.702
SWIGLU_LIMIT = 7.0
NEG = -1e30
TINY = 1e-37

LANES = 128
SUBLANES = 8
MXU_COLS = 256
VMEM_LIMIT = 56 * 1024 * 1024

PROJ_ROWS = 256
ATT_QBLOCK = 256
MID_ROWS = 256
MOE_ROWS = 512
MOE_COLS = 1024
DISPATCH_ROWS = 256
DISPATCH_GROUP = 8
COMBINE_ROWS = 256
COMBINE_GROUP = 8


def _rms(x, g):
    ms = jnp.mean(x * x, axis=-1, keepdims=True)
    return x * lax.rsqrt(ms + EPS) * g


def _fold_lanes(x, op):
    acc = x[:, :LANES]
    for c in range(LANES, x.shape[-1], LANES):
        acc = op(acc, x[:, c:c + LANES])
    return acc


def _params(*sem):
    return pltpu.CompilerParams(dimension_semantics=sem, vmem_limit_bytes=VMEM_LIMIT)


def _norm_proj_kernel(x_ref, g_ref, w_ref, *out_refs, col_chunk):
    h = _rms(x_ref[...], g_ref[...]).astype(BF16)
    c0 = 0
    for o_ref in out_refs:
        n = o_ref.shape[-1]
        for s in range(0, n, col_chunk):
            e = min(s + col_chunk, n)
            o_ref[:, s:e] = jnp.dot(h, w_ref[:, c0 + s:c0 + e],
                                    preferred_element_type=F32).astype(o_ref.dtype)
        c0 += n


def _norm_proj(x2d, g, w_bf, outs, rows):
    m, d = x2d.shape
    n_total = w_bf.shape[1]
    assert sum(n for n, _ in outs) == n_total and m % rows == 0
    return pl.pallas_call(
        functools.partial(_norm_proj_kernel, col_chunk=512),
        grid=(m // rows,),
        in_specs=[pl.BlockSpec((rows, d), lambda i: (i, 0)),
                  pl.BlockSpec((1, d), lambda i: (0, 0)),
                  pl.BlockSpec((d, n_total), lambda i: (0, 0), pipeline_mode=pl.Buffered(1))],
        out_specs=[pl.BlockSpec((rows, n), lambda i: (i, 0)) for n, _ in outs],
        out_shape=[jax.ShapeDtypeStruct((m, n), dt) for n, dt in outs],
        compiler_params=_params("parallel"),
        name="norm_proj",
    )(x2d, g.reshape(1, d), w_bf)


def _attn_kernel(q_ref, k_ref, v_ref, bias_ref, o_ref, kpad, vpad, *, seq, pad, qb, kb):
    hd = ATT_HEAD_DIM
    kpad[0:pad, :] = jnp.zeros((pad, LANES), BF16)
    vpad[0:pad, :] = jnp.zeros((pad, LANES), BF16)
    kpad[pad:pad + seq, :] = k_ref[0]
    vpad[pad:pad + seq, :] = v_ref[0]
    scale = hd ** -0.5
    kcol = lax.broadcasted_iota(I32, (qb, kb), 1)
    head_of_lane = lax.broadcasted_iota(I32, (qb, LANES), 1) // hd

    def block(ib, carry, *, near_start):
        s0 = pl.multiple_of(ib * qb, qb)
        q = q_ref[0, pl.ds(s0, qb), :] * scale
        kblk = kpad[pl.ds(s0, kb), :]
        vblk = vpad[pl.ds(s0, kb), :]
        out = None
        for hh in range(LANES // hd):
            mine = head_of_lane == hh
            s = lax.dot_general(jnp.where(mine, q, 0.0), kblk, (((1,), (1,)), ((), ())),
                                preferred_element_type=F32)
            s = s + bias_ref[hh]
            if near_start:
                s = jnp.where(kcol < (pad - s0), NEG, s)
            m = jnp.max(_fold_lanes(s, jnp.maximum), axis=-1, keepdims=True)
            p = jnp.exp(s - m)
            l = jnp.sum(_fold_lanes(p, jnp.add), axis=-1, keepdims=True)
            o = jnp.dot(p.astype(BF16), vblk, preferred_element_type=F32) / l
            out = o if out is None else jnp.where(mine, o, out)
        o_ref[0, pl.ds(s0, qb), :] = out.astype(o_ref.dtype)
        return carry

    n_near = min(pad // qb, seq // qb)
    lax.fori_loop(0, n_near, functools.partial(block, near_start=True), 0, unroll=2)
    lax.fori_loop(n_near, seq // qb, functools.partial(block, near_start=False), 0, unroll=3)


def _attention(qkv, rel_table, att_width):
    b, seq, _ = qkv.shape
    heads = att_width // ATT_HEAD_DIM
    hp = LANES // ATT_HEAD_DIM
    qb = min(ATT_QBLOCK, seq)
    pad = LEFT_CHUNKS * CHUNK
    kb = qb + pad
    ql = jnp.arange(qb)[:, None]
    kl = jnp.arange(kb)[None, :]
    span = qb + kb - 1
    rel_u = (qb - 1 + pad) - jnp.arange(span)
    u = rel_table[:, jnp.clip(rel_u, -MAX_REL, MAX_REL) + MAX_REL].astype(F32)
    u = jnp.concatenate([u, jnp.zeros((heads, 1), F32)], axis=1)
    skew = jnp.tile(u, (1, qb))[:, :qb * span].reshape(heads, qb, span)
    bias = skew[:, :, qb - 1:qb - 1 + kb]
    dchunk = ql // CHUNK + LEFT_CHUNKS - kl // CHUNK
    band = (dchunk >= 0) & (dchunk <= LEFT_CHUNKS)
    bias = jnp.where(band[None], bias, NEG)
    nblk = att_width // LANES
    return pl.pallas_call(
        functools.partial(_attn_kernel, seq=seq, pad=pad, qb=qb, kb=kb),
        grid=(b, heads // hp),
        in_specs=[pl.BlockSpec((1, seq, LANES), lambda i, j: (i, 0, j)),
                  pl.BlockSpec((1, seq, LANES), lambda i, j: (i, 0, nblk + j)),
                  pl.BlockSpec((1, seq, LANES), lambda i, j: (i, 0, 2 * nblk + j)),
                  pl.BlockSpec((hp, qb, kb), lambda i, j: (j, 0, 0))],
        out_specs=pl.BlockSpec((1, seq, LANES), lambda i, j: (i, 0, j)),
        out_shape=jax.ShapeDtypeStruct((b, seq, att_width), BF16),
        scratch_shapes=[pltpu.VMEM((seq + pad, LANES), BF16),
                        pltpu.VMEM((seq + pad, LANES), BF16)],
        compiler_params=_params("parallel", "parallel"),
        name="band_attention",
    )(qkv, qkv, qkv, bias)


def _lru_kernel(xr_ref, yg_ref, cw_ref, cb_ref, wg_ref, bg_ref, lam_ref, o_ref,
                a_scr, b_scr, h_scr, *, seq):
    bw = xr_ref.shape[-1]
    x = xr_ref[0]
    row = lax.broadcasted_iota(I32, (seq, bw), 0)
    cw = cw_ref[...]
    xc = cb_ref[...] + cw[CONV_WIDTH - 1:CONV_WIDTH, :] * x
    for j in range(1, CONV_WIDTH):
        xs = jnp.where(row >= j, pltpu.roll(x, j, 0), 0.0)
        xc = xc + cw[CONV_WIDTH - 1 - j:CONV_WIDTH - j, :] * xs
    gates = jnp.dot(xc.astype(BF16), wg_ref[0], preferred_element_type=F32) + bg_ref[0]
    r = 1.0 / (1.0 + jnp.exp(-gates[:, :bw]))
    i = 1.0 / (1.0 + jnp.exp(-gates[:, bw:]))
    z = -lam_ref[...]
    softplus = jnp.maximum(z, 0.0) + jnp.log1p(jnp.exp(-jnp.abs(z)))
    a = jnp.exp(-LRU_C * r * softplus)
    v = 1.0 - a * a
    b = v * lax.rsqrt(jnp.maximum(v, TINY)) * (i * xc)
    a_scr[...] = a
    b_scr[...] = b
    ng = seq // SUBLANES
    ga, gb = [], []
    for j in range(SUBLANES):
        aj = a_scr[pl.ds(j, ng, stride=SUBLANES), :]
        bj = b_scr[pl.ds(j, ng, stride=SUBLANES), :]
        if j:
            bj = aj * gb[-1] + bj
            aj = aj * ga[-1]
        ga.append(aj)
        gb.append(bj)
    ta, tb = ga[-1], gb[-1]
    grow = lax.broadcasted_iota(I32, (ng, bw), 0)
    k = 1
    while k < ng:
        keep = grow >= k
        tb = jnp.where(keep, ta * pltpu.roll(tb, k, 0) + tb, tb)
        ta = jnp.where(keep, ta * pltpu.roll(ta, k, 0), ta)
        k *= 2
    h_in = jnp.where(grow >= 1, pltpu.roll(tb, 1, 0), 0.0)
    for j in range(SUBLANES):
        h_scr[pl.ds(j, ng, stride=SUBLANES), :] = ga[j] * h_in + gb[j]
    g = yg_ref[0]
    gelu = 0.5 * g * (1.0 + jnp.tanh(0.7978845608028654 * (g + 0.044715 * (g * g * g))))
    o_ref[0] = (h_scr[...] * gelu).astype(o_ref.dtype)


def _lru(xy, conv_w, conv_b, w_ga, b_ga, w_gx, b_gx, lam, lru_width):
    b, seq, _ = xy.shape
    nb, bw, _ = w_ga.shape
    wg = jnp.concatenate([w_ga, w_gx], axis=-1).astype(BF16)
    bg = jnp.concatenate([b_ga, b_gx], axis=-1).reshape(nb, 1, 2 * bw)
    return pl.pallas_call(
        functools.partial(_lru_kernel, seq=seq),
        grid=(b, nb),
        in_specs=[pl.BlockSpec((1, seq, bw), lambda i, n: (i, 0, n)),
                  pl.BlockSpec((1, seq, bw), lambda i, n: (i, 0, nb + n)),
                  pl.BlockSpec((CONV_WIDTH, bw), lambda i, n: (0, n)),
                  pl.BlockSpec((1, bw), lambda i, n: (0, n)),
                  pl.BlockSpec((1, bw, 2 * bw), lambda i, n: (n, 0, 0)),
                  pl.BlockSpec((1, 1, 2 * bw), lambda i, n: (n, 0, 0)),
                  pl.BlockSpec((1, bw), lambda i, n: (0, n))],
        out_specs=pl.BlockSpec((1, seq, bw), lambda i, n: (i, 0, n)),
        out_shape=jax.ShapeDtypeStruct((b, seq, lru_width), BF16),
        scratch_shapes=[pltpu.VMEM((seq, bw), F32)] * 3,
        compiler_params=_params("parallel", "parallel"),
        name="rg_lru",
    )(xy, xy, conv_w, conv_b.reshape(1, -1), wg, bg, lam.reshape(1, -1))


def _pack_bf16_pair(lo, hi):
    lo_b = pltpu.bitcast(lo.astype(BF16).astype(F32), U32) >> 16
    hi_b = pltpu.bitcast(hi.astype(BF16).astype(F32), U32) & jnp.uint32(0xFFFF0000)
    return hi_b | lo_b


def _unpack_bf16_pair(w):
    lo = pltpu.bitcast(w << 16, F32)
    hi = pltpu.bitcast(w & jnp.uint32(0xFFFF0000), F32)
    return lo, hi


def _store_token_major(ref, x, nc=None):
    tokens = x.shape[0]
    nc = nc or x.shape[-1] // LANES
    for c in range(x.shape[-1] // LANES):
        ref[pl.ds(c, tokens, stride=nc), :] = x[:, c * LANES:(c + 1) * LANES]


def _load_token_major(load, tokens, nc):
    return jnp.concatenate([load(pl.ds(c, tokens, stride=nc)) for c in range(nc)], axis=-1)


def _mid_kernel(x_ref, att_ref, lru_ref, ck_ref, cv_ref, ga_ref, gl_ref, wo_ref, gc_ref, wcq_ref,
                wco_ref, gf_ref, wr_ref, br_ref,
                x2_ref, hp_ref, route_ref, gate_ref, cnt_ref, cnt_scr, *, n_experts):
    rows, d = x_ref.shape
    aw = att_ref.shape[-1]

    @pl.when((pl.program_id(0) == 0) & (pl.program_id(1) == 0))
    def _():
        cnt_scr[...] = jnp.zeros_like(cnt_scr)

    att_n = _rms(att_ref[...].astype(F32), ga_ref[...]).astype(BF16)
    lru_n = _rms(lru_ref[...].astype(F32), gl_ref[...]).astype(BF16)
    x1 = (x_ref[...]
          + jnp.dot(att_n, wo_ref[0:aw, :], preferred_element_type=F32)
          + jnp.dot(lru_n, wo_ref[aw:, :], preferred_element_type=F32))

    hq = _rms(x1, gc_ref[...]).astype(BF16)
    xw = wcq_ref.shape[-1]
    xhd = xw // X_HEADS
    cq = (jnp.dot(hq, wcq_ref[...], preferred_element_type=F32) * (xhd ** -0.5)).astype(BF16)
    ck = ck_ref[0]
    cv = cv_ref[0]
    heads = []
    for h in range(X_HEADS):
        sl = slice(h * xhd, (h + 1) * xhd)
        s = lax.dot_general(cq[:, sl], ck[:, sl], (((1,), (1,)), ((), ())),
                            preferred_element_type=F32)
        m = jnp.max(s, axis=-1, keepdims=True)
        p = jnp.exp(s - m)
        l = jnp.sum(p, axis=-1, keepdims=True)
        heads.append(jnp.dot(p.astype(BF16), cv[:, sl], preferred_element_type=F32) / l)
    o = jnp.concatenate(heads, axis=-1).astype(BF16)
    x2 = x1 + jnp.dot(o, wco_ref[...], preferred_element_type=F32)
    x2_ref[...] = x2

    hn = _rms(x2, gf_ref[...])
    _store_token_major(hp_ref, _pack_bf16_pair(hn[:, :d // 2], hn[:, d // 2:]))

    logits = jnp.dot(hn.astype(BF16), wr_ref[...], preferred_element_type=F32) + br_ref[...]
    lane = lax.broadcasted_iota(I32, (rows, LANES), 1).astype(F32)
    work = logits
    sel_e, sel_v = [], []
    onehot = jnp.zeros((rows, LANES), F32)
    for _ in range(TOP_K):
        v = jnp.max(work, axis=-1, keepdims=True)
        e = jnp.min(jnp.where(work == v, lane, float(LANES)), axis=-1, keepdims=True)
        hit = lane == e
        onehot = jnp.where(hit, 1.0, onehot)
        work = jnp.where(hit, NEG * 2, work)
        sel_e.append(e)
        sel_v.append(v)
    ex = [jnp.exp(v - sel_v[0]) for v in sel_v]
    den = ex[0] + ex[1] + ex[2] + ex[3]

    ri = lax.broadcasted_iota(I32, (rows, rows), 0)
    ci = lax.broadcasted_iota(I32, (rows, rows), 1)
    tri = jnp.where(ci < ri, 1.0, 0.0).astype(BF16)
    before = jnp.dot(tri, onehot.astype(BF16), preferred_element_type=F32) + cnt_scr[...]
    route = jnp.zeros((rows, LANES), F32)
    gate = jnp.zeros((rows, LANES), F32)
    for k in range(TOP_K):
        rank = jnp.sum(jnp.where(lane == sel_e[k], before, 0.0), axis=-1, keepdims=True)
        route = jnp.where(lane == float(k), sel_e[k], route)
        route = jnp.where(lane == float(TOP_K + k), rank, route)
        gate = jnp.where(lane == float(k), ex[k] / den, gate)
    route_ref[...] = route.astype(I32)
    gate_ref[...] = gate
    cnt_scr[...] = cnt_scr[...] + jnp.sum(onehot, axis=0, keepdims=True)
    cnt_ref[...] = cnt_scr[...]


def _mid(x, att, lru, ckv, ga, gl, wo_bf, gc, wcq_bf, wco_bf, gf, wr_pad, br_pad, n_experts, rows):
    b, seq, d = x.shape
    aw, lw = att.shape[-1], lru.shape[-1]
    mem_len, xw2 = ckv.shape[1], ckv.shape[2]
    xw = xw2 // 2
    n = b * seq
    nt = seq // rows
    row_map = lambda i, t: (i * nt + t, 0)
    const = lambda i, t: (0, 0)
    res = lambda shape: pl.BlockSpec(shape, const, pipeline_mode=pl.Buffered(1))
    return pl.pallas_call(
        functools.partial(_mid_kernel, n_experts=n_experts),
        grid=(b, nt),
        in_specs=[pl.BlockSpec((rows, d), row_map),
                  pl.BlockSpec((rows, aw), row_map),
                  pl.BlockSpec((rows, lw), row_map),
                  pl.BlockSpec((1, mem_len, xw), lambda i, t: (i, 0, 0)),
                  pl.BlockSpec((1, mem_len, xw), lambda i, t: (i, 0, 1)),
                  res((1, aw)), res((1, lw)), res((aw + lw, d)), res((1, d)), res((d, xw)),
                  res((xw, d)), res((1, d)), res((d, LANES)), res((1, LANES))],
        out_specs=[pl.BlockSpec((rows, d), row_map),
                   pl.BlockSpec((rows * (d // 2 // LANES), LANES), row_map),
                   pl.BlockSpec((rows, LANES), row_map),
                   pl.BlockSpec((rows, LANES), row_map),
                   pl.BlockSpec((1, LANES), const)],
        out_shape=[jax.ShapeDtypeStruct((n, d), F32),
                   jax.ShapeDtypeStruct((n * (d // 2 // LANES), LANES), U32),
                   jax.ShapeDtypeStruct((n, LANES), I32),
                   jax.ShapeDtypeStruct((n, LANES), F32),
                   jax.ShapeDtypeStruct((1, LANES), F32)],
        scratch_shapes=[pltpu.VMEM((1, LANES), F32)],
        compiler_params=_params("arbitrary", "arbitrary"),
        name="mix_cross_router",
    )(x.reshape(n, d), att.reshape(n, aw), lru.reshape(n, lw), ckv, ckv,
      ga.reshape(1, aw), gl.reshape(1, lw), wo_bf, gc.reshape(1, d), wcq_bf, wco_bf,
      gf.reshape(1, d), wr_pad, br_pad)


def _dispatch_kernel(fill_ref, dest_ref, src_ref, xs_ref, buf, zbuf, lsem, ssem, zsem,
                     *, rows, nc, n_fill, tail_chunk):
    i = pl.program_id(0)
    n = pl.num_programs(0)
    ztok = zbuf.shape[0] // nc

    def tokens(first, count):
        return pl.ds(pl.multiple_of(first * nc, nc), count * nc)

    def load(t, slot):
        return pltpu.make_async_copy(src_ref.at[tokens(t * rows, rows), :], buf.at[slot],
                                     lsem.at[slot])

    def zero_copy(start, size):
        return pltpu.make_async_copy(zbuf.at[pl.ds(0, size * nc), :],
                                     xs_ref.at[tokens(start, size), :], zsem)

    def fill_chunks(fn):
        for e in range(n_fill):
            start = fill_ref[e]
            length = fill_ref[n_fill + 1 + e]
            size = ztok
            while size >= 1:
                part = length & size
                @pl.when(part != 0)
                def _(start=start, size=size):
                    fn(zero_copy(start, size))
                start = start + part
                size //= 2
        tail_start = fill_ref[n_fill]
        n_tail = fill_ref[2 * n_fill + 1]

        def tail(c, carry):
            fn(zero_copy(tail_start + c * tail_chunk, tail_chunk))
            return carry

        lax.fori_loop(0, n_tail, tail, 0)

    @pl.when(i == 0)
    def _():
        load(0, 0).start()
        zbuf[...] = jnp.zeros_like(zbuf)
        fill_chunks(lambda c: c.start())
        fill_chunks(lambda c: c.wait())

    @pl.when(i + 1 < n)
    def _():
        load(i + 1, (i + 1) % 3).start()

    slot = i % 3
    par = i % 2
    load(i, slot).wait()

    def start_rows(g, c):
        for j in range(DISPATCH_GROUP):
            tok = g * DISPATCH_GROUP + j
            for k in range(TOP_K):
                dst = dest_ref[0, 0, tok * TOP_K + k]
                pltpu.make_async_copy(buf.at[slot, tokens(tok, 1), :],
                                      xs_ref.at[tokens(dst, 1), :], ssem.at[par]).start()
        return c

    lax.fori_loop(0, rows // DISPATCH_GROUP, start_rows, 0)

    def wait_tile(p):
        for _ in range(TOP_K):
            pltpu.make_async_copy(buf.at[0], xs_ref.at[pl.ds(0, rows * nc), :], ssem.at[p]).wait()

    @pl.when(i > 0)
    def _():
        wait_tile(1 - par)

    @pl.when(i == n - 1)
    def _():
        wait_tile(par)


def _dispatch(fill, dest, src, n_slots, rows, moe_rows, nc):
    n_tok = src.shape[0] // nc
    nt = n_tok // rows
    n_fill = (fill.shape[0] - 2) // 2
    ztok = moe_rows // 2
    return pl.pallas_call(
        functools.partial(_dispatch_kernel, rows=rows, nc=nc, n_fill=n_fill, tail_chunk=ztok),
        grid_spec=pltpu.PrefetchScalarGridSpec(
            num_scalar_prefetch=1,
            grid=(nt,),
            in_specs=[pl.BlockSpec((1, 1, rows * TOP_K), lambda i, fill: (i, 0, 0),
                                   memory_space=pltpu.SMEM),
                      pl.BlockSpec(memory_space=pl.ANY)],
            out_specs=pl.BlockSpec(memory_space=pl.ANY),
            scratch_shapes=[pltpu.VMEM((3, rows * nc, LANES), src.dtype),
                            pltpu.VMEM((ztok * nc, LANES), src.dtype),
                            pltpu.SemaphoreType.DMA((3,)),
                            pltpu.SemaphoreType.DMA((2,)),
                            pltpu.SemaphoreType.DMA(())]),
        out_shape=jax.ShapeDtypeStruct((n_slots * nc, LANES), src.dtype),
        compiler_params=_params("arbitrary"),
        name="dispatch_scatter",
    )(fill, dest.reshape(nt, 1, rows * TOP_K), src)


def _grouped_kernel(te_ref, meta_ref, x_ref, wa_ref, wb_ref, ba_ref, bb_ref, o_ref, wa_bf, wb_bf,
                    *, swiglu):
    i = pl.program_id(1)
    prev = te_ref[jnp.maximum(i - 1, 0)]
    new_expert = (i == 0) | (te_ref[i] != prev)

    @pl.when(new_expert)
    def _():
        wa_bf[...] = wa_ref[0].astype(BF16)
        wb_bf[...] = wb_ref[0].astype(BF16)

    @pl.when(i < meta_ref[0])
    def _():
        rows = o_ref.shape[0] if swiglu else x_ref.shape[0]
        if swiglu:
            nc = x_ref.shape[0] // rows
            lo, hi = _unpack_bf16_pair(_load_token_major(lambda s: x_ref[s, :], rows, nc))
            xb = jnp.concatenate([lo.astype(BF16), hi.astype(BF16)], axis=-1)
        else:
            xb = x_ref[...]
        cols = wa_bf.shape[-1]
        for c0 in range(0, cols, MXU_COLS):
            cs = slice(c0, c0 + MXU_COLS)
            a = jnp.dot(xb, wa_bf[:, cs], preferred_element_type=F32) + ba_ref[0, :, cs]
            b = jnp.dot(xb, wb_bf[:, cs], preferred_element_type=F32) + bb_ref[0, :, cs]
            if swiglu:
                gate = jnp.minimum(a, SWIGLU_LIMIT)
                up = jnp.clip(b, -SWIGLU_LIMIT, SWIGLU_LIMIT)
                act = (up + 1.0) * gate * (1.0 / (1.0 + jnp.exp(-SWIGLU_ALPHA * gate)))
                o_ref[:, cs] = act.astype(o_ref.dtype)
            else:
                packed = _pack_bf16_pair(a, b)
                for c in range(MXU_COLS // LANES):
                    o_ref[pl.ds(c0 // LANES + c, rows, stride=cols // LANES), :] = (
                        packed[:, c * LANES:(c + 1) * LANES])

    @pl.when(i >= meta_ref[0])
    def _():
        o_ref[...] = jnp.zeros_like(o_ref)


def _grouped_matmul(tile_expert, meta, x, w, bias, rows, cols, swiglu):
    n_exp, kdim, n2 = w.shape
    half = n2 // 2
    nj = half // cols
    used = lambda i, meta: jnp.minimum(i, meta[0] - 1)
    if swiglu:
        x_rows = rows * (kdim // 2 // LANES)
        n_slots = x.shape[0] // (kdim // 2 // LANES)
        out_spec = pl.BlockSpec((rows, cols), lambda j, i, te, meta: (i, j))
        out_shape = jax.ShapeDtypeStruct((n_slots, half), BF16)
    else:
        assert nj == 1
        x_rows = rows
        n_slots = x.shape[0]
        out_spec = pl.BlockSpec((rows * (half // LANES), LANES), lambda j, i, te, meta: (i, 0))
        out_shape = jax.ShapeDtypeStruct((n_slots * (half // LANES), LANES), U32)
    n_tiles = n_slots // rows
    return pl.pallas_call(
        functools.partial(_grouped_kernel, swiglu=swiglu),
        grid_spec=pltpu.PrefetchScalarGridSpec(
            num_scalar_prefetch=2,
            grid=(nj, n_tiles),
            in_specs=[pl.BlockSpec((x_rows, x.shape[1]), lambda j, i, te, meta: (used(i, meta), 0)),
                      pl.BlockSpec((1, kdim, cols), lambda j, i, te, meta: (te[i], 0, j)),
                      pl.BlockSpec((1, kdim, cols), lambda j, i, te, meta: (te[i], 0, nj + j)),
                      pl.BlockSpec((1, 1, cols), lambda j, i, te, meta: (te[i], 0, j)),
                      pl.BlockSpec((1, 1, cols), lambda j, i, te, meta: (te[i], 0, nj + j))],
            out_specs=out_spec,
            scratch_shapes=[pltpu.VMEM((kdim, cols), BF16), pltpu.VMEM((kdim, cols), BF16)]),
        out_shape=out_shape,
        compiler_params=_params("arbitrary", "arbitrary"),
        name="moe_gate_up" if swiglu else "moe_down",
    )(tile_expert, meta, x, w, w, bias.reshape(n_exp, 1, n2), bias.reshape(n_exp, 1, n2))


def _combine_kernel(dcur_ref, dnext_ref, y_ref, x2_ref, gate_ref, g_ref, o_ref, ybuf, sem, *, rows):
    i = pl.program_id(0)
    n = pl.num_programs(0)

    half = x2_ref.shape[-1] // 2
    nc = half // LANES
    cur = i % 2

    def token(t):
        return pl.ds(pl.multiple_of(t * nc, nc), nc)

    def start_tile(d_ref, slot):
        def step(g, c):
            for j in range(COMBINE_GROUP):
                tok = g * COMBINE_GROUP + j
                for k in range(TOP_K):
                    src = d_ref[0, 0, tok * TOP_K + k]
                    pltpu.make_async_copy(y_ref.at[token(src), :], ybuf.at[slot, k, token(tok), :],
                                          sem.at[slot]).start()
            return c

        lax.fori_loop(0, rows // COMBINE_GROUP, step, 0)

    @pl.when(i == 0)
    def _():
        start_tile(dcur_ref, 0)

    @pl.when(i + 1 < n)
    def _():
        start_tile(dnext_ref, 1 - cur)

    for k in range(TOP_K):
        pltpu.make_async_copy(y_ref.at[pl.ds(0, rows * nc), :], ybuf.at[cur, k],
                              sem.at[cur]).wait()

    gate = gate_ref[...]
    acc_lo = x2_ref[:, :half]
    acc_hi = x2_ref[:, half:]
    for k in range(TOP_K):
        lo, hi = _unpack_bf16_pair(
            _load_token_major(lambda s, k=k: ybuf[cur, k, s, :], rows, nc))
        gk = gate[:, k:k + 1]
        acc_lo = acc_lo + gk * lo
        acc_hi = acc_hi + gk * hi
    ms = (jnp.sum(acc_lo * acc_lo, axis=-1, keepdims=True)
          + jnp.sum(acc_hi * acc_hi, axis=-1, keepdims=True)) / (2 * half)
    inv = lax.rsqrt(ms + EPS)
    o_ref[:, :half] = acc_lo * inv * g_ref[:, :half]
    o_ref[:, half:] = acc_hi * inv * g_ref[:, half:]


def _combine(dest, y, x2, gate, g, rows):
    n, d = x2.shape
    nt = n // rows
    return pl.pallas_call(
        functools.partial(_combine_kernel, rows=rows),
        grid=(nt,),
        in_specs=[pl.BlockSpec((1, 1, rows * TOP_K), lambda i: (i, 0, 0), memory_space=pltpu.SMEM),
                  pl.BlockSpec((1, 1, rows * TOP_K), lambda i: (jnp.minimum(i + 1, nt - 1), 0, 0),
                               memory_space=pltpu.SMEM),
                  pl.BlockSpec(memory_space=pl.ANY),
                  pl.BlockSpec((rows, d), lambda i: (i, 0)),
                  pl.BlockSpec((rows, LANES), lambda i: (i, 0)),
                  pl.BlockSpec((1, d), lambda i: (0, 0))],
        out_specs=pl.BlockSpec((rows, d), lambda i: (i, 0)),
        out_shape=jax.ShapeDtypeStruct((n, d), F32),
        scratch_shapes=[pltpu.VMEM((2, TOP_K, rows * (d // 2 // LANES), LANES), U32),
                        pltpu.SemaphoreType.DMA((2,))],
        compiler_params=_params("arbitrary"),
        name="combine_norm",
    )(dest.reshape(nt, 1, rows * TOP_K), dest.reshape(nt, 1, rows * TOP_K), y, x2, gate,
      g.reshape(1, d))


def kernel(x, mem, norm_mix_g, w_in, rel_table, conv_w, conv_b, w_ga, b_ga, w_gx, b_gx, lru_lambda, norm_att_out_g, norm_lru_out_g, w_out, norm_cross_g, norm_mem_g, w_cq, w_ckv, w_co, norm_ffn_g, w_router, b_router, w_gu, b_gu, w_down, b_down, norm_final_g):
    b, seq, d = x.shape
    depth = w_in.shape[0]
    att_width = d // 2
    lru_width = d - att_width
    n_experts = w_router.shape[-1]
    n_tok = b * seq
    mem_len = mem.shape[1]
    assert depth == 1 and n_experts <= LANES and seq % min(ATT_QBLOCK, seq) == 0

    moe_rows = min(MOE_ROWS, n_tok)
    n_tiles = (n_tok * TOP_K + n_experts * (moe_rows - 1)) // moe_rows
    n_slots = n_tiles * moe_rows

    for l in range(depth):
        qkv, xy = _norm_proj(x.reshape(n_tok, d), norm_mix_g[l], w_in[l].astype(BF16),
                             [(3 * att_width, BF16), (2 * lru_width, F32)], min(PROJ_ROWS, seq))
        att = _attention(qkv.reshape(b, seq, 3 * att_width), rel_table[l], att_width)
        lru = _lru(xy.reshape(b, seq, 2 * lru_width), conv_w[l], conv_b[l], w_ga[l], b_ga[l],
                   w_gx[l], b_gx[l], lru_lambda[l], lru_width)
        (ckv,) = _norm_proj(mem.reshape(b * mem_len, d), norm_mem_g[l], w_ckv[l].astype(BF16),
                            [(w_ckv.shape[-1], BF16)], min(PROJ_ROWS, mem_len))
        ckv = ckv.reshape(b, mem_len, -1)
        wr_pad = jnp.zeros((d, LANES), F32).at[:, :n_experts].set(w_router[l]).astype(BF16)
        br_pad = jnp.full((1, LANES), NEG, F32).at[0, :n_experts].set(b_router[l])
        x2, hpack, route, gate, counts = _mid(
            x, att, lru, ckv, norm_att_out_g[l], norm_lru_out_g[l], w_out[l].astype(BF16),
            norm_cross_g[l], w_cq[l].astype(BF16), w_co[l].astype(BF16), norm_ffn_g[l],
            wr_pad, br_pad, n_experts, min(MID_ROWS, seq))
        counts = counts[0, :n_experts].astype(I32)
        padded = (counts + moe_rows - 1) // moe_rows * moe_rows
        pad_end = jnp.cumsum(padded)
        pad_start = pad_end - padded
        expert_ids = jnp.arange(n_experts, dtype=I32)
        sel_e = route[:, :TOP_K]
        start_of = jnp.sum(jnp.where(sel_e[:, :, None] == expert_ids, pad_start, 0), axis=-1)
        dest = start_of + route[:, TOP_K:2 * TOP_K]
        total = pad_end[-1]
        meta = (total // moe_rows).astype(I32).reshape(1)
        tile_start = jnp.minimum(jnp.arange(n_tiles, dtype=I32) * moe_rows, total - moe_rows)
        tile_expert = jnp.sum(tile_start[:, None] >= pad_end[None, :], axis=1).astype(I32)
        fill = jnp.concatenate([pad_start + counts, total[None], padded - counts,
                                ((n_slots - total) // (moe_rows // 2))[None]]).astype(I32)
        xs = _dispatch(fill, dest, hpack, n_slots, min(DISPATCH_ROWS, n_tok), moe_rows,
                       d // 2 // LANES)
        cols = min(MOE_COLS, w_down.shape[2] // 2)
        act = _grouped_matmul(tile_expert, meta, xs, w_gu[l], b_gu[l], moe_rows, cols, True)
        y = _grouped_matmul(tile_expert, meta, act, w_down[l], b_down[l], moe_rows,
                            min(MOE_COLS, d // 2), False)
        x = _combine(dest, y, x2, gate, norm_final_g, min(COMBINE_ROWS, n_tok)).reshape(b, seq, d)
    return x
```

```python
import functools

import jax
import jax.numpy as jnp
from jax import lax
from jax.experimental import pallas as pl
from jax.experimental.pallas import tpu as pltpu

F32 = jnp.float32
BF16 = jnp.bfloat16
U32 = jnp.uint32
I32 = jnp.int32

EPS = 1e-6
CHUNK = 64
LEFT_CHUNKS = 8
ATT_HEAD_DIM = 64
MAX_REL = 2 * CHUNK
CONV_WIDTH = 4
LRU_C = 8.0
X_HEADS = 4
TOP_K = 4
SWIGLU_ALPHA = 1.702
SWIGLU_LIMIT = 7.0
NEG = -1e30
TINY = 1e-37

LANES = 128
SUBLANES = 8
MXU_COLS = 256
DMA_PRIORITIES = 2
VMEM_LIMIT = 56 * 1024 * 1024

PROJ_ROWS = 256
ATT_QBLOCK = 256
MID_ROWS = 256
MOE_ROWS = 512
MOE_COLS = 1024
CAST_ROWS = 64
DISPATCH_ROWS = 256
DISPATCH_GROUP = 8
COMBINE_ROWS = 256
COMBINE_GROUP = 8


def _rms(x, g):
    ms = jnp.mean(x * x, axis=-1, keepdims=True)
    return x * lax.rsqrt(ms + EPS) * g


def _fold_lanes(x, op):
    acc = x[:, :LANES]
    for c in range(LANES, x.shape[-1], LANES):
        acc = op(acc, x[:, c:c + LANES])
    return acc


def _params(*sem):
    return pltpu.CompilerParams(dimension_semantics=sem, vmem_limit_bytes=VMEM_LIMIT)


def _norm_proj_kernel(x_ref, g_ref, w_ref, *out_refs, col_chunk):
    h = _rms(x_ref[...], g_ref[...]).astype(BF16)
    c0 = 0
    for o_ref in out_refs:
        n = o_ref.shape[-1]
        for s in range(0, n, col_chunk):
            e = min(s + col_chunk, n)
            o_ref[:, s:e] = jnp.dot(h, w_ref[:, c0 + s:c0 + e],
                                    preferred_element_type=F32).astype(o_ref.dtype)
        c0 += n


def _norm_proj(x2d, g, w_bf, outs, rows):
    m, d = x2d.shape
    n_total = w_bf.shape[1]
    assert sum(n for n, _ in outs) == n_total and m % rows == 0
    return pl.pallas_call(
        functools.partial(_norm_proj_kernel, col_chunk=512),
        grid=(m // rows,),
        in_specs=[pl.BlockSpec((rows, d), lambda i: (i, 0)),
                  pl.BlockSpec((1, d), lambda i: (0, 0)),
                  pl.BlockSpec((d, n_total), lambda i: (0, 0), pipeline_mode=pl.Buffered(1))],
        out_specs=[pl.BlockSpec((rows, n), lambda i: (i, 0)) for n, _ in outs],
        out_shape=[jax.ShapeDtypeStruct((m, n), dt) for n, dt in outs],
        compiler_params=_params("parallel"),
        name="norm_proj",
    )(x2d, g.reshape(1, d), w_bf)


def _attn_kernel(q_ref, k_ref, v_ref, bias_ref, o_ref, kpad, vpad, *, seq, pad, qb, kb):
    hd = ATT_HEAD_DIM
    kpad[0:pad, :] = jnp.zeros((pad, LANES), BF16)
    vpad[0:pad, :] = jnp.zeros((pad, LANES), BF16)
    kpad[pad:pad + seq, :] = k_ref[0]
    vpad[pad:pad + seq, :] = v_ref[0]
    scale = hd ** -0.5
    kcol = lax.broadcasted_iota(I32, (qb, kb), 1)
    head_of_lane = lax.broadcasted_iota(I32, (qb, LANES), 1) // hd

    def block(ib, carry, *, near_start):
        s0 = pl.multiple_of(ib * qb, qb)
        q = q_ref[0, pl.ds(s0, qb), :] * scale
        kblk = kpad[pl.ds(s0, kb), :]
        vblk = vpad[pl.ds(s0, kb), :]
        out = None
        for hh in range(LANES // hd):
            mine = head_of_lane == hh
            s = lax.dot_general(jnp.where(mine, q, 0.0), kblk, (((1,), (1,)), ((), ())),
                                preferred_element_type=F32)
            s = s + bias_ref[hh]
            if near_start:
                s = jnp.where(kcol < (pad - s0), NEG, s)
            m = jnp.max(_fold_lanes(s, jnp.maximum), axis=-1, keepdims=True)
            p = jnp.exp(s - m)
            l = jnp.sum(_fold_lanes(p, jnp.add), axis=-1, keepdims=True)
            o = jnp.dot(p.astype(BF16), vblk, preferred_element_type=F32) / l
            out = o if out is None else jnp.where(mine, o, out)
        o_ref[0, pl.ds(s0, qb), :] = out.astype(o_ref.dtype)
        return carry

    n_near = min(pad // qb, seq // qb)
    lax.fori_loop(0, n_near, functools.partial(block, near_start=True), 0, unroll=2)
    lax.fori_loop(n_near, seq // qb, functools.partial(block, near_start=False), 0, unroll=3)


def _attention(qkv, rel_table, att_width):
    b, seq, _ = qkv.shape
    heads = att_width // ATT_HEAD_DIM
    hp = LANES // ATT_HEAD_DIM
    qb = min(ATT_QBLOCK, seq)
    pad = LEFT_CHUNKS * CHUNK
    kb = qb + pad
    ql = jnp.arange(qb)[:, None]
    kl = jnp.arange(kb)[None, :]
    span = qb + kb - 1
    rel_u = (qb - 1 + pad) - jnp.arange(span)
    u = rel_table[:, jnp.clip(rel_u, -MAX_REL, MAX_REL) + MAX_REL].astype(F32)
    u = jnp.concatenate([u, jnp.zeros((heads, 1), F32)], axis=1)
    skew = jnp.tile(u, (1, qb))[:, :qb * span].reshape(heads, qb, span)
    bias = skew[:, :, qb - 1:qb - 1 + kb]
    dchunk = ql // CHUNK + LEFT_CHUNKS - kl // CHUNK
    band = (dchunk >= 0) & (dchunk <= LEFT_CHUNKS)
    bias = jnp.where(band[None], bias, NEG)
    nblk = att_width // LANES
    return pl.pallas_call(
        functools.partial(_attn_kernel, seq=seq, pad=pad, qb=qb, kb=kb),
        grid=(b, heads // hp),
        in_specs=[pl.BlockSpec((1, seq, LANES), lambda i, j: (i, 0, j)),
                  pl.BlockSpec((1, seq, LANES), lambda i, j: (i, 0, nblk + j)),
                  pl.BlockSpec((1, seq, LANES), lambda i, j: (i, 0, 2 * nblk + j)),
                  pl.BlockSpec((hp, qb, kb), lambda i, j: (j, 0, 0))],
        out_specs=pl.BlockSpec((1, seq, LANES), lambda i, j: (i, 0, j)),
        out_shape=jax.ShapeDtypeStruct((b, seq, att_width), BF16),
        scratch_shapes=[pltpu.VMEM((seq + pad, LANES), BF16),
                        pltpu.VMEM((seq + pad, LANES), BF16)],
        compiler_params=_params("parallel", "parallel"),
        name="band_attention",
    )(qkv, qkv, qkv, bias)


def _lru_kernel(xr_ref, yg_ref, cw_ref, cb_ref, wg_ref, bg_ref, lam_ref, o_ref,
                a_scr, b_scr, h_scr, *, seq):
    bw = xr_ref.shape[-1]
    x = xr_ref[0]
    row = lax.broadcasted_iota(I32, (seq, bw), 0)
    cw = cw_ref[...]
    xc = cb_ref[...] + cw[CONV_WIDTH - 1:CONV_WIDTH, :] * x
    for j in range(1, CONV_WIDTH):
        xs = jnp.where(row >= j, pltpu.roll(x, j, 0), 0.0)
        xc = xc + cw[CONV_WIDTH - 1 - j:CONV_WIDTH - j, :] * xs
    gates = jnp.dot(xc.astype(BF16), wg_ref[0], preferred_element_type=F32) + bg_ref[0]
    r = 1.0 / (1.0 + jnp.exp(-gates[:, :bw]))
    i = 1.0 / (1.0 + jnp.exp(-gates[:, bw:]))
    z = -lam_ref[...]
    softplus = jnp.maximum(z, 0.0) + jnp.log1p(jnp.exp(-jnp.abs(z)))
    a = jnp.exp(-LRU_C * r * softplus)
    v = 1.0 - a * a
    b = v * lax.rsqrt(jnp.maximum(v, TINY)) * (i * xc)
    a_scr[...] = a
    b_scr[...] = b
    ng = seq // SUBLANES
    ga, gb = [], []
    for j in range(SUBLANES):
        aj = a_scr[pl.ds(j, ng, stride=SUBLANES), :]
        bj = b_scr[pl.ds(j, ng, stride=SUBLANES), :]
        if j:
            bj = aj * gb[-1] + bj
            aj = aj * ga[-1]
        ga.append(aj)
        gb.append(bj)
    ta, tb = ga[-1], gb[-1]
    grow = lax.broadcasted_iota(I32, (ng, bw), 0)
    k = 1
    while k < ng:
        keep = grow >= k
        tb = jnp.where(keep, ta * pltpu.roll(tb, k, 0) + tb, tb)
        ta = jnp.where(keep, ta * pltpu.roll(ta, k, 0), ta)
        k *= 2
    h_in = jnp.where(grow >= 1, pltpu.roll(tb, 1, 0), 0.0)
    for j in range(SUBLANES):
        h_scr[pl.ds(j, ng, stride=SUBLANES), :] = ga[j] * h_in + gb[j]
    g = yg_ref[0]
    gelu = 0.5 * g * (1.0 + jnp.tanh(0.7978845608028654 * (g + 0.044715 * (g * g * g))))
    o_ref[0] = (h_scr[...] * gelu).astype(o_ref.dtype)


def _lru(xy, conv_w, conv_b, w_ga, b_ga, w_gx, b_gx, lam, lru_width):
    b, seq, _ = xy.shape
    nb, bw, _ = w_ga.shape
    wg = jnp.concatenate([w_ga, w_gx], axis=-1).astype(BF16)
    bg = jnp.concatenate([b_ga, b_gx], axis=-1).reshape(nb, 1, 2 * bw)
    return pl.pallas_call(
        functools.partial(_lru_kernel, seq=seq),
        grid=(b, nb),
        in_specs=[pl.BlockSpec((1, seq, bw), lambda i, n: (i, 0, n)),
                  pl.BlockSpec((1, seq, bw), lambda i, n: (i, 0, nb + n)),
                  pl.BlockSpec((CONV_WIDTH, bw), lambda i, n: (0, n)),
                  pl.BlockSpec((1, bw), lambda i, n: (0, n)),
                  pl.BlockSpec((1, bw, 2 * bw), lambda i, n: (n, 0, 0)),
                  pl.BlockSpec((1, 1, 2 * bw), lambda i, n: (n, 0, 0)),
                  pl.BlockSpec((1, bw), lambda i, n: (0, n))],
        out_specs=pl.BlockSpec((1, seq, bw), lambda i, n: (i, 0, n)),
        out_shape=jax.ShapeDtypeStruct((b, seq, lru_width), BF16),
        scratch_shapes=[pltpu.VMEM((seq, bw), F32)] * 3,
        compiler_params=_params("parallel", "parallel"),
        name="rg_lru",
    )(xy, xy, conv_w, conv_b.reshape(1, -1), wg, bg, lam.reshape(1, -1))


def _pack_bf16_pair(lo, hi):
    lo_b = pltpu.bitcast(lo.astype(BF16).astype(F32), U32) >> 16
    hi_b = pltpu.bitcast(hi.astype(BF16).astype(F32), U32) & jnp.uint32(0xFFFF0000)
    return hi_b | lo_b


def _unpack_bf16_pair(w):
    lo = pltpu.bitcast(w << 16, F32)
    hi = pltpu.bitcast(w & jnp.uint32(0xFFFF0000), F32)
    return lo, hi


def _store_token_major(ref, x, nc=None):
    tokens = x.shape[0]
    nc = nc or x.shape[-1] // LANES
    for c in range(x.shape[-1] // LANES):
        ref[pl.ds(c, tokens, stride=nc), :] = x[:, c * LANES:(c + 1) * LANES]


def _load_token_major(load, tokens, nc):
    return jnp.concatenate([load(pl.ds(c, tokens, stride=nc)) for c in range(nc)], axis=-1)


def _mid_kernel(x_ref, att_ref, lru_ref, ck_ref, cv_ref, ga_ref, gl_ref, wo_ref, gc_ref, wcq_ref,
                wco_ref, gf_ref, wr_ref, br_ref,
                x2_ref, hp_ref, route_ref, gate_ref, cnt_ref, cnt_scr, *, n_experts):
    rows, d = x_ref.shape
    aw = att_ref.shape[-1]

    @pl.when((pl.program_id(0) == 0) & (pl.program_id(1) == 0))
    def _():
        cnt_scr[...] = jnp.zeros_like(cnt_scr)

    att_n = _rms(att_ref[...].astype(F32), ga_ref[...]).astype(BF16)
    lru_n = _rms(lru_ref[...].astype(F32), gl_ref[...]).astype(BF16)
    x1 = (x_ref[...]
          + jnp.dot(att_n, wo_ref[0:aw, :], preferred_element_type=F32)
          + jnp.dot(lru_n, wo_ref[aw:, :], preferred_element_type=F32))

    hq = _rms(x1, gc_ref[...]).astype(BF16)
    xw = wcq_ref.shape[-1]
    xhd = xw // X_HEADS
    cq = (jnp.dot(hq, wcq_ref[...], preferred_element_type=F32) * (xhd ** -0.5)).astype(BF16)
    ck = ck_ref[0]
    cv = cv_ref[0]
    heads = []
    for h in range(X_HEADS):
        sl = slice(h * xhd, (h + 1) * xhd)
        s = lax.dot_general(cq[:, sl], ck[:, sl], (((1,), (1,)), ((), ())),
                            preferred_element_type=F32)
        m = jnp.max(s, axis=-1, keepdims=True)
        p = jnp.exp(s - m)
        l = jnp.sum(p, axis=-1, keepdims=True)
        heads.append(jnp.dot(p.astype(BF16), cv[:, sl], preferred_element_type=F32) / l)
    o = jnp.concatenate(heads, axis=-1).astype(BF16)
    x2 = x1 + jnp.dot(o, wco_ref[...], preferred_element_type=F32)
    x2_ref[...] = x2

    hn = _rms(x2, gf_ref[...])
    _store_token_major(hp_ref, _pack_bf16_pair(hn[:, :d // 2], hn[:, d // 2:]))

    logits = jnp.dot(hn.astype(BF16), wr_ref[...], preferred_element_type=F32) + br_ref[...]
    lane = lax.broadcasted_iota(I32, (rows, LANES), 1).astype(F32)
    work = logits
    sel_e, sel_v = [], []
    onehot = jnp.zeros((rows, LANES), F32)
    for _ in range(TOP_K):
        v = jnp.max(work, axis=-1, keepdims=True)
        e = jnp.min(jnp.where(work == v, lane, float(LANES)), axis=-1, keepdims=True)
        hit = lane == e
        onehot = jnp.where(hit, 1.0, onehot)
        work = jnp.where(hit, NEG * 2, work)
        sel_e.append(e)
        sel_v.append(v)
    ex = [jnp.exp(v - sel_v[0]) for v in sel_v]
    den = ex[0] + ex[1] + ex[2] + ex[3]

    ri = lax.broadcasted_iota(I32, (rows, rows), 0)
    ci = lax.broadcasted_iota(I32, (rows, rows), 1)
    tri = jnp.where(ci < ri, 1.0, 0.0).astype(BF16)
    before = jnp.dot(tri, onehot.astype(BF16), preferred_element_type=F32) + cnt_scr[...]
    route = jnp.zeros((rows, LANES), F32)
    gate = jnp.zeros((rows, LANES), F32)
    for k in range(TOP_K):
        rank = jnp.sum(jnp.where(lane == sel_e[k], before, 0.0), axis=-1, keepdims=True)
        route = jnp.where(lane == float(k), sel_e[k], route)
        route = jnp.where(lane == float(TOP_K + k), rank, route)
        gate = jnp.where(lane == float(k), ex[k] / den, gate)
    route_ref[...] = route.astype(I32)
    gate_ref[...] = gate
    cnt_scr[...] = cnt_scr[...] + jnp.sum(onehot, axis=0, keepdims=True)
    cnt_ref[...] = cnt_scr[...]


def _mid(x, att, lru, ckv, ga, gl, wo_bf, gc, wcq_bf, wco_bf, gf, wr_pad, br_pad, n_experts, rows):
    b, seq, d = x.shape
    aw, lw = att.shape[-1], lru.shape[-1]
    mem_len, xw2 = ckv.shape[1], ckv.shape[2]
    xw = xw2 // 2
    n = b * seq
    nt = seq // rows
    row_map = lambda i, t: (i * nt + t, 0)
    const = lambda i, t: (0, 0)
    res = lambda shape: pl.BlockSpec(shape, const, pipeline_mode=pl.Buffered(1))
    return pl.pallas_call(
        functools.partial(_mid_kernel, n_experts=n_experts),
        grid=(b, nt),
        in_specs=[pl.BlockSpec((rows, d), row_map),
                  pl.BlockSpec((rows, aw), row_map),
                  pl.BlockSpec((rows, lw), row_map),
                  pl.BlockSpec((1, mem_len, xw), lambda i, t: (i, 0, 0)),
                  pl.BlockSpec((1, mem_len, xw), lambda i, t: (i, 0, 1)),
                  res((1, aw)), res((1, lw)), res((aw + lw, d)), res((1, d)), res((d, xw)),
                  res((xw, d)), res((1, d)), res((d, LANES)), res((1, LANES))],
        out_specs=[pl.BlockSpec((rows, d), row_map),
                   pl.BlockSpec((rows * (d // 2 // LANES), LANES), row_map),
                   pl.BlockSpec((rows, LANES), row_map),
                   pl.BlockSpec((rows, LANES), row_map),
                   pl.BlockSpec((1, LANES), const)],
        out_shape=[jax.ShapeDtypeStruct((n, d), F32),
                   jax.ShapeDtypeStruct((n * (d // 2 // LANES), LANES), U32),
                   jax.ShapeDtypeStruct((n, LANES), I32),
                   jax.ShapeDtypeStruct((n, LANES), F32),
                   jax.ShapeDtypeStruct((1, LANES), F32)],
        scratch_shapes=[pltpu.VMEM((1, LANES), F32)],
        compiler_params=_params("arbitrary", "arbitrary"),
        name="mix_cross_router",
    )(x.reshape(n, d), att.reshape(n, aw), lru.reshape(n, lw), ckv, ckv,
      ga.reshape(1, aw), gl.reshape(1, lw), wo_bf, gc.reshape(1, d), wcq_bf, wco_bf,
      gf.reshape(1, d), wr_pad, br_pad)


def _dispatch_kernel(fill_ref, dest_ref, src_ref, xs_ref, buf, zbuf, lsem, ssem, zsem,
                     *, rows, nc, n_fill, tail_chunk):
    i = pl.program_id(0)
    n = pl.num_programs(0)
    ztok = zbuf.shape[0] // nc

    def tokens(first, count):
        return pl.ds(pl.multiple_of(first * nc, nc), count * nc)

    def load(t, slot):
        return pltpu.make_async_copy(src_ref.at[tokens(t * rows, rows), :], buf.at[slot],
                                     lsem.at[slot])

    def zero_copy(start, size):
        return pltpu.make_async_copy(zbuf.at[pl.ds(0, size * nc), :],
                                     xs_ref.at[tokens(start, size), :], zsem)

    def fill_chunks(fn):
        for e in range(n_fill):
            start = fill_ref[e]
            length = fill_ref[n_fill + 1 + e]
            size = ztok
            while size >= 1:
                part = length & size
                @pl.when(part != 0)
                def _(start=start, size=size):
                    fn(zero_copy(start, size))
                start = start + part
                size //= 2
        tail_start = fill_ref[n_fill]
        n_tail = fill_ref[2 * n_fill + 1]

        def tail(c, carry):
            fn(zero_copy(tail_start + c * tail_chunk, tail_chunk))
            return carry

        lax.fori_loop(0, n_tail, tail, 0)

    @pl.when(i == 0)
    def _():
        load(0, 0).start()
        zbuf[...] = jnp.zeros_like(zbuf)
        fill_chunks(lambda c: c.start())
        fill_chunks(lambda c: c.wait())

    @pl.when(i + 1 < n)
    def _():
        load(i + 1, (i + 1) % 3).start()

    slot = i % 3
    par = i % 2
    load(i, slot).wait()

    def start_rows(g, c):
        for j in range(DISPATCH_GROUP):
            tok = g * DISPATCH_GROUP + j
            for k in range(TOP_K):
                dst = dest_ref[0, 0, tok * TOP_K + k]
                pltpu.make_async_copy(buf.at[slot, tokens(tok, 1), :],
                                      xs_ref.at[tokens(dst, 1), :],
                                      ssem.at[par]).start(priority=k % DMA_PRIORITIES)
        return c

    lax.fori_loop(0, rows // DISPATCH_GROUP, start_rows, 0)

    def wait_tile(p):
        for _ in range(TOP_K):
            pltpu.make_async_copy(buf.at[0], xs_ref.at[pl.ds(0, rows * nc), :], ssem.at[p]).wait()

    @pl.when(i > 0)
    def _():
        wait_tile(1 - par)

    @pl.when(i == n - 1)
    def _():
        wait_tile(par)


def _dispatch(fill, dest, src, n_slots, rows, moe_rows, nc):
    n_tok = src.shape[0] // nc
    nt = n_tok // rows
    n_fill = (fill.shape[0] - 2) // 2
    ztok = moe_rows // 2
    return pl.pallas_call(
        functools.partial(_dispatch_kernel, rows=rows, nc=nc, n_fill=n_fill, tail_chunk=ztok),
        grid_spec=pltpu.PrefetchScalarGridSpec(
            num_scalar_prefetch=1,
            grid=(nt,),
            in_specs=[pl.BlockSpec((1, 1, rows * TOP_K), lambda i, fill: (i, 0, 0),
                                   memory_space=pltpu.SMEM),
                      pl.BlockSpec(memory_space=pl.ANY)],
            out_specs=pl.BlockSpec(memory_space=pl.ANY),
            scratch_shapes=[pltpu.VMEM((3, rows * nc, LANES), src.dtype),
                            pltpu.VMEM((ztok * nc, LANES), src.dtype),
                            pltpu.SemaphoreType.DMA((3,)),
                            pltpu.SemaphoreType.DMA((2,)),
                            pltpu.SemaphoreType.DMA(())]),
        out_shape=jax.ShapeDtypeStruct((n_slots * nc, LANES), src.dtype),
        compiler_params=_params("arbitrary"),
        name="dispatch_scatter",
    )(fill, dest.reshape(nt, 1, rows * TOP_K), src)


def _grouped_kernel(te_ref, nxt_ref, meta_ref, x_ref, w_ref, ba_ref, bb_ref, o_ref,
                    w_f32, wa_bf, wb_bf, wsem, *, swiglu):
    j = pl.program_id(0)
    i = pl.program_id(1)
    nj = pl.num_programs(0)
    cols = wa_bf.shape[-1]
    expert = te_ref[i]
    new_expert = (i == 0) | (expert != te_ref[jnp.maximum(i - 1, 0)])

    def weight_copy(jj, ee, which):
        col = pl.multiple_of((which * nj + jj) * cols, cols)
        return pltpu.make_async_copy(w_ref.at[ee, :, pl.ds(col, cols)], w_f32.at[which],
                                     wsem.at[which])

    def start_weights(jj, ee):
        weight_copy(jj, ee, 0).start()
        weight_copy(jj, ee, 1).start()

    @pl.when((j == 0) & (i == 0))
    def _():
        start_weights(0, expert)

    @pl.when(new_expert)
    def _():
        weight_copy(j, expert, 0).wait()
        weight_copy(j, expert, 1).wait()
        def cast_rows(c, carry):
            rs = pl.ds(pl.multiple_of(c * CAST_ROWS, CAST_ROWS), CAST_ROWS)
            wa_bf[rs, :] = w_f32[0, rs, :].astype(BF16)
            wb_bf[rs, :] = w_f32[1, rs, :].astype(BF16)
            return carry

        lax.fori_loop(0, wa_bf.shape[0] // CAST_ROWS, cast_rows, 0)
        following = nxt_ref[i]

        @pl.when(following >= 0)
        def _():
            start_weights(j, following)

        @pl.when((following < 0) & (j + 1 < nj))
        def _():
            start_weights(j + 1, te_ref[0])

    @pl.when(i < meta_ref[0])
    def _():
        rows = o_ref.shape[0] if swiglu else x_ref.shape[0]
        if swiglu:
            nc = x_ref.shape[0] // rows
            lo, hi = _unpack_bf16_pair(_load_token_major(lambda s: x_ref[s, :], rows, nc))
            xb = jnp.concatenate([lo.astype(BF16), hi.astype(BF16)], axis=-1)
        else:
            xb = x_ref[...]
        for c0 in range(0, cols, MXU_COLS):
            cs = slice(c0, c0 + MXU_COLS)
            a = jnp.dot(xb, wa_bf[:, cs], preferred_element_type=F32) + ba_ref[0, :, cs]
            b = jnp.dot(xb, wb_bf[:, cs], preferred_element_type=F32) + bb_ref[0, :, cs]
            if swiglu:
                gate = jnp.minimum(a, SWIGLU_LIMIT)
                up = jnp.clip(b, -SWIGLU_LIMIT, SWIGLU_LIMIT)
                act = (up + 1.0) * gate * (1.0 / (1.0 + jnp.exp(-SWIGLU_ALPHA * gate)))
                o_ref[:, cs] = act.astype(o_ref.dtype)
            else:
                packed = _pack_bf16_pair(a, b)
                for c in range(MXU_COLS // LANES):
                    o_ref[pl.ds(c0 // LANES + c, rows, stride=cols // LANES), :] = (
                        packed[:, c * LANES:(c + 1) * LANES])

    @pl.when(i >= meta_ref[0])
    def _():
        o_ref[...] = jnp.zeros_like(o_ref)


def _grouped_matmul(tile_expert, tile_next, meta, x, w, bias, rows, cols, swiglu):
    n_exp, kdim, n2 = w.shape
    half = n2 // 2
    nj = half // cols
    used = lambda i, meta: jnp.minimum(i, meta[0] - 1)
    if swiglu:
        x_rows = rows * (kdim // 2 // LANES)
        n_slots = x.shape[0] // (kdim // 2 // LANES)
        out_spec = pl.BlockSpec((rows, cols), lambda j, i, te, nx, meta: (i, j))
        out_shape = jax.ShapeDtypeStruct((n_slots, half), BF16)
    else:
        assert nj == 1
        x_rows = rows
        n_slots = x.shape[0]
        out_spec = pl.BlockSpec((rows * (half // LANES), LANES), lambda j, i, te, nx, meta: (i, 0))
        out_shape = jax.ShapeDtypeStruct((n_slots * (half // LANES), LANES), U32)
    n_tiles = n_slots // rows
    return pl.pallas_call(
        functools.partial(_grouped_kernel, swiglu=swiglu),
        grid_spec=pltpu.PrefetchScalarGridSpec(
            num_scalar_prefetch=3,
            grid=(nj, n_tiles),
            in_specs=[pl.BlockSpec((x_rows, x.shape[1]),
                                   lambda j, i, te, nx, meta: (used(i, meta), 0)),
                      pl.BlockSpec(memory_space=pl.ANY),
                      pl.BlockSpec((1, 1, cols), lambda j, i, te, nx, meta: (te[i], 0, j)),
                      pl.BlockSpec((1, 1, cols), lambda j, i, te, nx, meta: (te[i], 0, nj + j))],
            out_specs=out_spec,
            scratch_shapes=[pltpu.VMEM((2, kdim, cols), F32),
                            pltpu.VMEM((kdim, cols), BF16), pltpu.VMEM((kdim, cols), BF16),
                            pltpu.SemaphoreType.DMA((2,))]),
        out_shape=out_shape,
        compiler_params=_params("arbitrary", "arbitrary"),
        name="moe_gate_up" if swiglu else "moe_down",
    )(tile_expert, tile_next, meta, x, w, bias.reshape(n_exp, 1, n2), bias.reshape(n_exp, 1, n2))


def _combine_kernel(dcur_ref, dnext_ref, y_ref, x2_ref, gate_ref, g_ref, o_ref, ybuf, sem, *, rows):
    i = pl.program_id(0)
    n = pl.num_programs(0)

    half = x2_ref.shape[-1] // 2
    nc = half // LANES
    cur = i % 2

    def token(t):
        return pl.ds(pl.multiple_of(t * nc, nc), nc)

    def start_tile(d_ref, slot):
        def step(g, c):
            for j in range(COMBINE_GROUP):
                tok = g * COMBINE_GROUP + j
                for k in range(TOP_K):
                    src = d_ref[0, 0, tok * TOP_K + k]
                    pltpu.make_async_copy(y_ref.at[token(src), :], ybuf.at[slot, k, token(tok), :],
                                          sem.at[slot]).start(priority=k % DMA_PRIORITIES)
            return c

        lax.fori_loop(0, rows // COMBINE_GROUP, step, 0)

    @pl.when(i == 0)
    def _():
        start_tile(dcur_ref, 0)

    @pl.when(i + 1 < n)
    def _():
        start_tile(dnext_ref, 1 - cur)

    for k in range(TOP_K):
        pltpu.make_async_copy(y_ref.at[pl.ds(0, rows * nc), :], ybuf.at[cur, k],
                              sem.at[cur]).wait()

    gate = gate_ref[...]
    acc_lo = x2_ref[:, :half]
    acc_hi = x2_ref[:, half:]
    for k in range(TOP_K):
        lo, hi = _unpack_bf16_pair(
            _load_token_major(lambda s, k=k: ybuf[cur, k, s, :], rows, nc))
        gk = gate[:, k:k + 1]
        acc_lo = acc_lo + gk * lo
        acc_hi = acc_hi + gk * hi
    ms = (jnp.sum(acc_lo * acc_lo, axis=-1, keepdims=True)
          + jnp.sum(acc_hi * acc_hi, axis=-1, keepdims=True)) / (2 * half)
    inv = lax.rsqrt(ms + EPS)
    o_ref[:, :half] = acc_lo * inv * g_ref[:, :half]
    o_ref[:, half:] = acc_hi * inv * g_ref[:, half:]


def _combine(dest, y, x2, gate, g, rows):
    n, d = x2.shape
    nt = n // rows
    return pl.pallas_call(
        functools.partial(_combine_kernel, rows=rows),
        grid=(nt,),
        in_specs=[pl.BlockSpec((1, 1, rows * TOP_K), lambda i: (i, 0, 0), memory_space=pltpu.SMEM),
                  pl.BlockSpec((1, 1, rows * TOP_K), lambda i: (jnp.minimum(i + 1, nt - 1), 0, 0),
                               memory_space=pltpu.SMEM),
                  pl.BlockSpec(memory_space=pl.ANY),
                  pl.BlockSpec((rows, d), lambda i: (i, 0)),
                  pl.BlockSpec((rows, LANES), lambda i: (i, 0)),
                  pl.BlockSpec((1, d), lambda i: (0, 0))],
        out_specs=pl.BlockSpec((rows, d), lambda i: (i, 0)),
        out_shape=jax.ShapeDtypeStruct((n, d), F32),
        scratch_shapes=[pltpu.VMEM((2, TOP_K, rows * (d // 2 // LANES), LANES), U32),
                        pltpu.SemaphoreType.DMA((2,))],
        compiler_params=_params("arbitrary"),
        name="combine_norm",
    )(dest.reshape(nt, 1, rows * TOP_K), dest.reshape(nt, 1, rows * TOP_K), y, x2, gate,
      g.reshape(1, d))


def kernel(x, mem, norm_mix_g, w_in, rel_table, conv_w, conv_b, w_ga, b_ga, w_gx, b_gx, lru_lambda, norm_att_out_g, norm_lru_out_g, w_out, norm_cross_g, norm_mem_g, w_cq, w_ckv, w_co, norm_ffn_g, w_router, b_router, w_gu, b_gu, w_down, b_down, norm_final_g):
    b, seq, d = x.shape
    depth = w_in.shape[0]
    att_width = d // 2
    lru_width = d - att_width
    n_experts = w_router.shape[-1]
    n_tok = b * seq
    mem_len = mem.shape[1]
    assert depth == 1 and n_experts <= LANES and seq % min(ATT_QBLOCK, seq) == 0

    moe_rows = min(MOE_ROWS, n_tok)
    n_tiles = (n_tok * TOP_K + n_experts * (moe_rows - 1)) // moe_rows
    n_slots = n_tiles * moe_rows

    for l in range(depth):
        qkv, xy = _norm_proj(x.reshape(n_tok, d), norm_mix_g[l], w_in[l].astype(BF16),
                             [(3 * att_width, BF16), (2 * lru_width, F32)], min(PROJ_ROWS, seq))
        att = _attention(qkv.reshape(b, seq, 3 * att_width), rel_table[l], att_width)
        lru = _lru(xy.reshape(b, seq, 2 * lru_width), conv_w[l], conv_b[l], w_ga[l], b_ga[l],
                   w_gx[l], b_gx[l], lru_lambda[l], lru_width)
        (ckv,) = _norm_proj(mem.reshape(b * mem_len, d), norm_mem_g[l], w_ckv[l].astype(BF16),
                            [(w_ckv.shape[-1], BF16)], min(PROJ_ROWS, mem_len))
        ckv = ckv.reshape(b, mem_len, -1)
        wr_pad = jnp.zeros((d, LANES), F32).at[:, :n_experts].set(w_router[l]).astype(BF16)
        br_pad = jnp.full((1, LANES), NEG, F32).at[0, :n_experts].set(b_router[l])
        x2, hpack, route, gate, counts = _mid(
            x, att, lru, ckv, norm_att_out_g[l], norm_lru_out_g[l], w_out[l].astype(BF16),
            norm_cross_g[l], w_cq[l].astype(BF16), w_co[l].astype(BF16), norm_ffn_g[l],
            wr_pad, br_pad, n_experts, min(MID_ROWS, seq))
        counts = counts[0, :n_experts].astype(I32)
        padded = (counts + moe_rows - 1) // moe_rows * moe_rows
        pad_end = jnp.cumsum(padded)
        pad_start = pad_end - padded
        expert_ids = jnp.arange(n_experts, dtype=I32)
        sel_e = route[:, :TOP_K]
        start_of = jnp.sum(jnp.where(sel_e[:, :, None] == expert_ids, pad_start, 0), axis=-1)
        dest = start_of + route[:, TOP_K:2 * TOP_K]
        total = pad_end[-1]
        meta = (total // moe_rows).astype(I32).reshape(1)
        tile_start = jnp.minimum(jnp.arange(n_tiles, dtype=I32) * moe_rows, total - moe_rows)
        tile_expert = jnp.sum(tile_start[:, None] >= pad_end[None, :], axis=1).astype(I32)
        later = (expert_ids[None, :] > expert_ids[:, None]) & (padded[None, :] > 0)
        next_expert = jnp.min(jnp.where(later, expert_ids[None, :], n_experts), axis=1)
        next_expert = jnp.where(next_expert == n_experts, -1, next_expert)
        tile_next = jnp.sum(jnp.where(tile_expert[:, None] == expert_ids, next_expert, 0),
                            axis=1).astype(I32)
        fill = jnp.concatenate([pad_start + counts, total[None], padded - counts,
                                ((n_slots - total) // (moe_rows // 2))[None]]).astype(I32)
        xs = _dispatch(fill, dest, hpack, n_slots, min(DISPATCH_ROWS, n_tok), moe_rows,
                       d // 2 // LANES)
        cols = min(MOE_COLS, w_down.shape[2] // 2)
        act = _grouped_matmul(tile_expert, tile_next, meta, xs, w_gu[l], b_gu[l], moe_rows, cols,
                              True)
        y = _grouped_matmul(tile_expert, tile_next, meta, act, w_down[l], b_down[l], moe_rows,
                            min(MOE_COLS, d // 2), False)
        x = _combine(dest, y, x2, gate, norm_final_g, min(COMBINE_ROWS, n_tok)).reshape(b, seq, d)
    return x
```

```python
import functools

import jax
import jax.numpy as jnp
from jax import lax
from jax.experimental import pallas as pl
from jax.experimental.pallas import tpu as pltpu

F32 = jnp.float32
BF16 = jnp.bfloat16
U32 = jnp.uint32
I32 = jnp.int32

EPS = 1e-6
CHUNK = 64
LEFT_CHUNKS = 8
ATT_HEAD_DIM = 64
MAX_REL = 2 * CHUNK
CONV_WIDTH = 4
LRU_C = 8.0
X_HEADS = 4
TOP_K = 4
SWIGLU_ALPHA = 1.702
SWIGLU_LIMIT = 7.0
NEG = -1e30
TINY = 1e-37

LANES = 128
SUBLANES = 8
MXU_COLS = 256
DMA_PRIORITIES = 2
VMEM_LIMIT = 56 * 1024 * 1024

PROJ_ROWS = 256
ATT_QBLOCK = 256
MID_ROWS = 512
MOE_ROWS = 512
MOE_COLS = 1024
CAST_ROWS = 64
MOE_SHORT_ROWS = 128
DISPATCH_ROWS = 256
DISPATCH_GROUP = 8
COMBINE_ROWS = 256
COMBINE_GROUP = 8


def _rms(x, g):
    ms = jnp.mean(x * x, axis=-1, keepdims=True)
    return x * lax.rsqrt(ms + EPS) * g


def _fold_lanes(x, op):
    acc = x[:, :LANES]
    for c in range(LANES, x.shape[-1], LANES):
        acc = op(acc, x[:, c:c + LANES])
    return acc


def _params(*sem):
    return pltpu.CompilerParams(dimension_semantics=sem, vmem_limit_bytes=VMEM_LIMIT)


def _norm_proj_kernel(x_ref, g_ref, w_ref, *out_refs, col_chunk):
    h = _rms(x_ref[...], g_ref[...]).astype(BF16)
    c0 = 0
    for o_ref in out_refs:
        n = o_ref.shape[-1]
        for s in range(0, n, col_chunk):
            e = min(s + col_chunk, n)
            o_ref[:, s:e] = jnp.dot(h, w_ref[:, c0 + s:c0 + e],
                                    preferred_element_type=F32).astype(o_ref.dtype)
        c0 += n


def _norm_proj(x2d, g, w_bf, outs, rows):
    m, d = x2d.shape
    n_total = w_bf.shape[1]
    assert sum(n for n, _ in outs) == n_total and m % rows == 0
    return pl.pallas_call(
        functools.partial(_norm_proj_kernel, col_chunk=512),
        grid=(m // rows,),
        in_specs=[pl.BlockSpec((rows, d), lambda i: (i, 0)),
                  pl.BlockSpec((1, d), lambda i: (0, 0)),
                  pl.BlockSpec((d, n_total), lambda i: (0, 0), pipeline_mode=pl.Buffered(1))],
        out_specs=[pl.BlockSpec((rows, n), lambda i: (i, 0)) for n, _ in outs],
        out_shape=[jax.ShapeDtypeStruct((m, n), dt) for n, dt in outs],
        compiler_params=_params("parallel"),
        name="norm_proj",
    )(x2d, g.reshape(1, d), w_bf)


def _attn_kernel(q_ref, k_ref, v_ref, bias_ref, o_ref, kpad, vpad, *, seq, pad, qb, kb):
    hd = ATT_HEAD_DIM
    kpad[0:pad, :] = jnp.zeros((pad, LANES), BF16)
    vpad[0:pad, :] = jnp.zeros((pad, LANES), BF16)
    kpad[pad:pad + seq, :] = k_ref[0]
    vpad[pad:pad + seq, :] = v_ref[0]
    scale = hd ** -0.5
    kcol = lax.broadcasted_iota(I32, (qb, kb), 1)
    head_of_lane = lax.broadcasted_iota(I32, (qb, LANES), 1) // hd

    def block(ib, carry, *, near_start):
        s0 = pl.multiple_of(ib * qb, qb)
        q = q_ref[0, pl.ds(s0, qb), :] * scale
        kblk = kpad[pl.ds(s0, kb), :]
        vblk = vpad[pl.ds(s0, kb), :]
        out = None
        for hh in range(LANES // hd):
            mine = head_of_lane == hh
            s = lax.dot_general(jnp.where(mine, q, 0.0), kblk, (((1,), (1,)), ((), ())),
                                preferred_element_type=F32)
            s = s + bias_ref[hh]
            if near_start:
                s = jnp.where(kcol < (pad - s0), NEG, s)
            m = jnp.max(_fold_lanes(s, jnp.maximum), axis=-1, keepdims=True)
            p = jnp.exp(s - m)
            l = jnp.sum(_fold_lanes(p, jnp.add), axis=-1, keepdims=True)
            o = jnp.dot(p.astype(BF16), vblk, preferred_element_type=F32) / l
            out = o if out is None else jnp.where(mine, o, out)
        o_ref[0, pl.ds(s0, qb), :] = out.astype(o_ref.dtype)
        return carry

    n_near = min(pad // qb, seq // qb)
    lax.fori_loop(0, n_near, functools.partial(block, near_start=True), 0, unroll=2)
    lax.fori_loop(n_near, seq // qb, functools.partial(block, near_start=False), 0, unroll=3)


def _attention(qkv, rel_table, att_width):
    b, seq, _ = qkv.shape
    heads = att_width // ATT_HEAD_DIM
    hp = LANES // ATT_HEAD_DIM
    qb = min(ATT_QBLOCK, seq)
    pad = LEFT_CHUNKS * CHUNK
    kb = qb + pad
    ql = jnp.arange(qb)[:, None]
    kl = jnp.arange(kb)[None, :]
    span = qb + kb - 1
    rel_u = (qb - 1 + pad) - jnp.arange(span)
    u = rel_table[:, jnp.clip(rel_u, -MAX_REL, MAX_REL) + MAX_REL].astype(F32)
    u = jnp.concatenate([u, jnp.zeros((heads, 1), F32)], axis=1)
    skew = jnp.tile(u, (1, qb))[:, :qb * span].reshape(heads, qb, span)
    bias = skew[:, :, qb - 1:qb - 1 + kb]
    dchunk = ql // CHUNK + LEFT_CHUNKS - kl // CHUNK
    band = (dchunk >= 0) & (dchunk <= LEFT_CHUNKS)
    bias = jnp.where(band[None], bias, NEG)
    nblk = att_width // LANES
    return pl.pallas_call(
        functools.partial(_attn_kernel, seq=seq, pad=pad, qb=qb, kb=kb),
        grid=(b, heads // hp),
        in_specs=[pl.BlockSpec((1, seq, LANES), lambda i, j: (i, 0, j)),
                  pl.BlockSpec((1, seq, LANES), lambda i, j: (i, 0, nblk + j)),
                  pl.BlockSpec((1, seq, LANES), lambda i, j: (i, 0, 2 * nblk + j)),
                  pl.BlockSpec((hp, qb, kb), lambda i, j: (j, 0, 0))],
        out_specs=pl.BlockSpec((1, seq, LANES), lambda i, j: (i, 0, j)),
        out_shape=jax.ShapeDtypeStruct((b, seq, att_width), BF16),
        scratch_shapes=[pltpu.VMEM((seq + pad, LANES), BF16),
                        pltpu.VMEM((seq + pad, LANES), BF16)],
        compiler_params=_params("parallel", "parallel"),
        name="band_attention",
    )(qkv, qkv, qkv, bias)


def _lru_kernel(xr_ref, yg_ref, cw_ref, cb_ref, wg_ref, bg_ref, lam_ref, o_ref,
                a_scr, b_scr, h_scr, *, seq):
    bw = xr_ref.shape[-1]
    x = xr_ref[0]
    row = lax.broadcasted_iota(I32, (seq, bw), 0)
    cw = cw_ref[...]
    xc = cb_ref[...] + cw[CONV_WIDTH - 1:CONV_WIDTH, :] * x
    for j in range(1, CONV_WIDTH):
        xs = jnp.where(row >= j, pltpu.roll(x, j, 0), 0.0)
        xc = xc + cw[CONV_WIDTH - 1 - j:CONV_WIDTH - j, :] * xs
    gates = jnp.dot(xc.astype(BF16), wg_ref[0], preferred_element_type=F32) + bg_ref[0]
    r = 1.0 / (1.0 + jnp.exp(-gates[:, :bw]))
    i = 1.0 / (1.0 + jnp.exp(-gates[:, bw:]))
    z = -lam_ref[...]
    softplus = jnp.maximum(z, 0.0) + jnp.log1p(jnp.exp(-jnp.abs(z)))
    a = jnp.exp(-LRU_C * r * softplus)
    v = 1.0 - a * a
    b = v * lax.rsqrt(jnp.maximum(v, TINY)) * (i * xc)
    a_scr[...] = a
    b_scr[...] = b
    ng = seq // SUBLANES
    ga, gb = [], []
    for j in range(SUBLANES):
        aj = a_scr[pl.ds(j, ng, stride=SUBLANES), :]
        bj = b_scr[pl.ds(j, ng, stride=SUBLANES), :]
        if j:
            bj = aj * gb[-1] + bj
            aj = aj * ga[-1]
        ga.append(aj)
        gb.append(bj)
    ta, tb = ga[-1], gb[-1]
    grow = lax.broadcasted_iota(I32, (ng, bw), 0)
    k = 1
    while k < ng:
        keep = grow >= k
        tb = jnp.where(keep, ta * pltpu.roll(tb, k, 0) + tb, tb)
        ta = jnp.where(keep, ta * pltpu.roll(ta, k, 0), ta)
        k *= 2
    h_in = jnp.where(grow >= 1, pltpu.roll(tb, 1, 0), 0.0)
    for j in range(SUBLANES):
        h_scr[pl.ds(j, ng, stride=SUBLANES), :] = ga[j] * h_in + gb[j]
    g = yg_ref[0]
    gelu = 0.5 * g * (1.0 + jnp.tanh(0.7978845608028654 * (g + 0.044715 * (g * g * g))))
    o_ref[0] = (h_scr[...] * gelu).astype(o_ref.dtype)


def _lru(xy, conv_w, conv_b, w_ga, b_ga, w_gx, b_gx, lam, lru_width):
    b, seq, _ = xy.shape
    nb, bw, _ = w_ga.shape
    wg = jnp.concatenate([w_ga, w_gx], axis=-1).astype(BF16)
    bg = jnp.concatenate([b_ga, b_gx], axis=-1).reshape(nb, 1, 2 * bw)
    return pl.pallas_call(
        functools.partial(_lru_kernel, seq=seq),
        grid=(b, nb),
        in_specs=[pl.BlockSpec((1, seq, bw), lambda i, n: (i, 0, n)),
                  pl.BlockSpec((1, seq, bw), lambda i, n: (i, 0, nb + n)),
                  pl.BlockSpec((CONV_WIDTH, bw), lambda i, n: (0, n)),
                  pl.BlockSpec((1, bw), lambda i, n: (0, n)),
                  pl.BlockSpec((1, bw, 2 * bw), lambda i, n: (n, 0, 0)),
                  pl.BlockSpec((1, 1, 2 * bw), lambda i, n: (n, 0, 0)),
                  pl.BlockSpec((1, bw), lambda i, n: (0, n))],
        out_specs=pl.BlockSpec((1, seq, bw), lambda i, n: (i, 0, n)),
        out_shape=jax.ShapeDtypeStruct((b, seq, lru_width), BF16),
        scratch_shapes=[pltpu.VMEM((seq, bw), F32)] * 3,
        compiler_params=_params("parallel", "parallel"),
        name="rg_lru",
    )(xy, xy, conv_w, conv_b.reshape(1, -1), wg, bg, lam.reshape(1, -1))


def _pack_bf16_pair(lo, hi):
    lo_b = pltpu.bitcast(lo.astype(BF16).astype(F32), U32) >> 16
    hi_b = pltpu.bitcast(hi.astype(BF16).astype(F32), U32) & jnp.uint32(0xFFFF0000)
    return hi_b | lo_b


def _unpack_bf16_pair(w):
    lo = pltpu.bitcast(w << 16, F32)
    hi = pltpu.bitcast(w & jnp.uint32(0xFFFF0000), F32)
    return lo, hi


def _store_token_major(ref, x, nc=None):
    tokens = x.shape[0]
    nc = nc or x.shape[-1] // LANES
    for c in range(x.shape[-1] // LANES):
        ref[pl.ds(c, tokens, stride=nc), :] = x[:, c * LANES:(c + 1) * LANES]


def _load_token_major(load, tokens, nc):
    return jnp.concatenate([load(pl.ds(c, tokens, stride=nc)) for c in range(nc)], axis=-1)


def _mid_kernel(x_ref, att_ref, lru_ref, ck_ref, cv_ref, ga_ref, gl_ref, wo_ref, gc_ref, wcq_ref,
                wco_ref, gf_ref, wr_ref, br_ref,
                x2_ref, hp_ref, route_ref, gate_ref, cnt_ref, cnt_scr, *, n_experts):
    rows, d = x_ref.shape
    aw = att_ref.shape[-1]

    @pl.when((pl.program_id(0) == 0) & (pl.program_id(1) == 0))
    def _():
        cnt_scr[...] = jnp.zeros_like(cnt_scr)

    att_n = _rms(att_ref[...].astype(F32), ga_ref[...]).astype(BF16)
    lru_n = _rms(lru_ref[...].astype(F32), gl_ref[...]).astype(BF16)
    x1 = (x_ref[...]
          + jnp.dot(att_n, wo_ref[0:aw, :], preferred_element_type=F32)
          + jnp.dot(lru_n, wo_ref[aw:, :], preferred_element_type=F32))

    hq = _rms(x1, gc_ref[...]).astype(BF16)
    xw = wcq_ref.shape[-1]
    xhd = xw // X_HEADS
    cq = (jnp.dot(hq, wcq_ref[...], preferred_element_type=F32) * (xhd ** -0.5)).astype(BF16)
    ck = ck_ref[0]
    cv = cv_ref[0]
    heads = []
    for h in range(X_HEADS):
        sl = slice(h * xhd, (h + 1) * xhd)
        s = lax.dot_general(cq[:, sl], ck[:, sl], (((1,), (1,)), ((), ())),
                            preferred_element_type=F32)
        m = jnp.max(s, axis=-1, keepdims=True)
        p = jnp.exp(s - m)
        l = jnp.sum(p, axis=-1, keepdims=True)
        heads.append(jnp.dot(p.astype(BF16), cv[:, sl], preferred_element_type=F32) / l)
    o = jnp.concatenate(heads, axis=-1).astype(BF16)
    x2 = x1 + jnp.dot(o, wco_ref[...], preferred_element_type=F32)
    x2_ref[...] = x2

    hn = _rms(x2, gf_ref[...])
    _store_token_major(hp_ref, _pack_bf16_pair(hn[:, :d // 2], hn[:, d // 2:]))

    logits = jnp.dot(hn.astype(BF16), wr_ref[...], preferred_element_type=F32) + br_ref[...]
    lane = lax.broadcasted_iota(I32, (rows, LANES), 1).astype(F32)
    work = logits
    sel_e, sel_v = [], []
    onehot = jnp.zeros((rows, LANES), F32)
    for _ in range(TOP_K):
        v = jnp.max(work, axis=-1, keepdims=True)
        e = jnp.min(jnp.where(work == v, lane, float(LANES)), axis=-1, keepdims=True)
        hit = lane == e
        onehot = jnp.where(hit, 1.0, onehot)
        work = jnp.where(hit, NEG * 2, work)
        sel_e.append(e)
        sel_v.append(v)
    ex = [jnp.exp(v - sel_v[0]) for v in sel_v]
    den = ex[0] + ex[1] + ex[2] + ex[3]

    ri = lax.broadcasted_iota(I32, (rows, rows), 0)
    ci = lax.broadcasted_iota(I32, (rows, rows), 1)
    tri = jnp.where(ci < ri, 1.0, 0.0).astype(BF16)
    before = jnp.dot(tri, onehot.astype(BF16), preferred_element_type=F32) + cnt_scr[...]
    route = jnp.zeros((rows, LANES), F32)
    gate = jnp.zeros((rows, LANES), F32)
    for k in range(TOP_K):
        rank = jnp.sum(jnp.where(lane == sel_e[k], before, 0.0), axis=-1, keepdims=True)
        route = jnp.where(lane == float(k), sel_e[k], route)
        route = jnp.where(lane == float(TOP_K + k), rank, route)
        gate = jnp.where(lane == float(k), ex[k] / den, gate)
    route_ref[...] = route.astype(I32)
    gate_ref[...] = gate
    cnt_scr[...] = cnt_scr[...] + jnp.sum(onehot, axis=0, keepdims=True)
    cnt_ref[...] = cnt_scr[...]


def _mid(x, att, lru, ckv, ga, gl, wo_bf, gc, wcq_bf, wco_bf, gf, wr_pad, br_pad, n_experts, rows):
    b, seq, d = x.shape
    aw, lw = att.shape[-1], lru.shape[-1]
    mem_len, xw2 = ckv.shape[1], ckv.shape[2]
    xw = xw2 // 2
    n = b * seq
    nt = seq // rows
    row_map = lambda i, t: (i * nt + t, 0)
    const = lambda i, t: (0, 0)
    res = lambda shape: pl.BlockSpec(shape, const, pipeline_mode=pl.Buffered(1))
    return pl.pallas_call(
        functools.partial(_mid_kernel, n_experts=n_experts),
        grid=(b, nt),
        in_specs=[pl.BlockSpec((rows, d), row_map),
                  pl.BlockSpec((rows, aw), row_map),
                  pl.BlockSpec((rows, lw), row_map),
                  pl.BlockSpec((1, mem_len, xw), lambda i, t: (i, 0, 0)),
                  pl.BlockSpec((1, mem_len, xw), lambda i, t: (i, 0, 1)),
                  res((1, aw)), res((1, lw)), res((aw + lw, d)), res((1, d)), res((d, xw)),
                  res((xw, d)), res((1, d)), res((d, LANES)), res((1, LANES))],
        out_specs=[pl.BlockSpec((rows, d), row_map),
                   pl.BlockSpec((rows * (d // 2 // LANES), LANES), row_map),
                   pl.BlockSpec((rows, LANES), row_map),
                   pl.BlockSpec((rows, LANES), row_map),
                   pl.BlockSpec((1, LANES), const)],
        out_shape=[jax.ShapeDtypeStruct((n, d), F32),
                   jax.ShapeDtypeStruct((n * (d // 2 // LANES), LANES), U32),
                   jax.ShapeDtypeStruct((n, LANES), I32),
                   jax.ShapeDtypeStruct((n, LANES), F32),
                   jax.ShapeDtypeStruct((1, LANES), F32)],
        scratch_shapes=[pltpu.VMEM((1, LANES), F32)],
        compiler_params=_params("arbitrary", "arbitrary"),
        name="mix_cross_router",
    )(x.reshape(n, d), att.reshape(n, aw), lru.reshape(n, lw), ckv, ckv,
      ga.reshape(1, aw), gl.reshape(1, lw), wo_bf, gc.reshape(1, d), wcq_bf, wco_bf,
      gf.reshape(1, d), wr_pad, br_pad)


def _dispatch_kernel(fill_ref, dest_ref, src_ref, xs_ref, buf, zbuf, lsem, ssem, zsem,
                     *, rows, nc, n_fill, tail_chunk):
    i = pl.program_id(0)
    n = pl.num_programs(0)
    ztok = zbuf.shape[0] // nc

    def tokens(first, count):
        return pl.ds(pl.multiple_of(first * nc, nc), count * nc)

    def load(t, slot):
        return pltpu.make_async_copy(src_ref.at[tokens(t * rows, rows), :], buf.at[slot],
                                     lsem.at[slot])

    def zero_copy(start, size):
        return pltpu.make_async_copy(zbuf.at[pl.ds(0, size * nc), :],
                                     xs_ref.at[tokens(start, size), :], zsem)

    def fill_chunks(fn):
        for e in range(n_fill):
            start = fill_ref[e]
            length = fill_ref[n_fill + 1 + e]
            size = ztok
            while size >= 1:
                part = length & size
                @pl.when(part != 0)
                def _(start=start, size=size):
                    fn(zero_copy(start, size))
                start = start + part
                size //= 2
        tail_start = fill_ref[n_fill]
        n_tail = fill_ref[2 * n_fill + 1]

        def tail(c, carry):
            fn(zero_copy(tail_start + c * tail_chunk, tail_chunk))
            return carry

        lax.fori_loop(0, n_tail, tail, 0)

    @pl.when(i == 0)
    def _():
        load(0, 0).start()
        zbuf[...] = jnp.zeros_like(zbuf)
        fill_chunks(lambda c: c.start())
        fill_chunks(lambda c: c.wait())

    @pl.when(i + 1 < n)
    def _():
        load(i + 1, (i + 1) % 3).start()

    slot = i % 3
    par = i % 2
    load(i, slot).wait()

    def start_rows(g, c):
        for j in range(DISPATCH_GROUP):
            tok = g * DISPATCH_GROUP + j
            for k in range(TOP_K):
                dst = dest_ref[0, 0, tok * TOP_K + k]
                pltpu.make_async_copy(buf.at[slot, tokens(tok, 1), :],
                                      xs_ref.at[tokens(dst, 1), :],
                                      ssem.at[par]).start(priority=k % DMA_PRIORITIES)
        return c

    lax.fori_loop(0, rows // DISPATCH_GROUP, start_rows, 0)

    def wait_tile(p):
        for _ in range(TOP_K):
            pltpu.make_async_copy(buf.at[0], xs_ref.at[pl.ds(0, rows * nc), :], ssem.at[p]).wait()

    @pl.when(i > 0)
    def _():
        wait_tile(1 - par)

    @pl.when(i == n - 1)
    def _():
        wait_tile(par)


def _dispatch(fill, dest, src, n_slots, rows, moe_rows, nc):
    n_tok = src.shape[0] // nc
    nt = n_tok // rows
    n_fill = (fill.shape[0] - 2) // 2
    ztok = moe_rows // 2
    return pl.pallas_call(
        functools.partial(_dispatch_kernel, rows=rows, nc=nc, n_fill=n_fill, tail_chunk=ztok),
        grid_spec=pltpu.PrefetchScalarGridSpec(
            num_scalar_prefetch=1,
            grid=(nt,),
            in_specs=[pl.BlockSpec((1, 1, rows * TOP_K), lambda i, fill: (i, 0, 0),
                                   memory_space=pltpu.SMEM),
                      pl.BlockSpec(memory_space=pl.ANY)],
            out_specs=pl.BlockSpec(memory_space=pl.ANY),
            scratch_shapes=[pltpu.VMEM((3, rows * nc, LANES), src.dtype),
                            pltpu.VMEM((ztok * nc, LANES), src.dtype),
                            pltpu.SemaphoreType.DMA((3,)),
                            pltpu.SemaphoreType.DMA((2,)),
                            pltpu.SemaphoreType.DMA(())]),
        out_shape=jax.ShapeDtypeStruct((n_slots * nc, LANES), src.dtype),
        compiler_params=_params("arbitrary"),
        name="dispatch_scatter",
    )(fill, dest.reshape(nt, 1, rows * TOP_K), src)


def _grouped_kernel(te_ref, nxt_ref, valid_ref, meta_ref, x_ref, w_ref, ba_ref, bb_ref, o_ref,
                    w_f32, wa_bf, wb_bf, wsem, *, swiglu):
    j = pl.program_id(0)
    i = pl.program_id(1)
    nj = pl.num_programs(0)
    cols = wa_bf.shape[-1]
    expert = te_ref[i]
    new_expert = (i == 0) | (expert != te_ref[jnp.maximum(i - 1, 0)])

    def weight_copy(jj, ee, which):
        col = pl.multiple_of((which * nj + jj) * cols, cols)
        return pltpu.make_async_copy(w_ref.at[ee, :, pl.ds(col, cols)], w_f32.at[which],
                                     wsem.at[which])

    def start_weights(jj, ee):
        weight_copy(jj, ee, 0).start()
        weight_copy(jj, ee, 1).start()

    @pl.when((j == 0) & (i == 0))
    def _():
        start_weights(0, expert)

    @pl.when(new_expert)
    def _():
        weight_copy(j, expert, 0).wait()
        weight_copy(j, expert, 1).wait()
        def cast_rows(c, carry):
            rs = pl.ds(pl.multiple_of(c * CAST_ROWS, CAST_ROWS), CAST_ROWS)
            wa_bf[rs, :] = w_f32[0, rs, :].astype(BF16)
            wb_bf[rs, :] = w_f32[1, rs, :].astype(BF16)
            return carry

        lax.fori_loop(0, wa_bf.shape[0] // CAST_ROWS, cast_rows, 0)
        following = nxt_ref[i]

        @pl.when(following >= 0)
        def _():
            start_weights(j, following)

        @pl.when((following < 0) & (j + 1 < nj))
        def _():
            start_weights(j + 1, te_ref[0])

    tile_rows = o_ref.shape[0] if swiglu else x_ref.shape[0]
    nc_in = x_ref.shape[0] // tile_rows
    nc_out = o_ref.shape[0] // tile_rows

    def compute(rows):
        if swiglu:
            lo, hi = _unpack_bf16_pair(_load_token_major(lambda s: x_ref[s, :], rows, nc_in))
            xb = jnp.concatenate([lo.astype(BF16), hi.astype(BF16)], axis=-1)
        else:
            xb = x_ref[0:rows, :]
        for c0 in range(0, cols, MXU_COLS):
            cs = slice(c0, c0 + MXU_COLS)
            a = jnp.dot(xb, wa_bf[:, cs], preferred_element_type=F32) + ba_ref[0, :, cs]
            b = jnp.dot(xb, wb_bf[:, cs], preferred_element_type=F32) + bb_ref[0, :, cs]
            if swiglu:
                gate = jnp.minimum(a, SWIGLU_LIMIT)
                up = jnp.clip(b, -SWIGLU_LIMIT, SWIGLU_LIMIT)
                act = (up + 1.0) * gate * (1.0 / (1.0 + jnp.exp(-SWIGLU_ALPHA * gate)))
                o_ref[0:rows, cs] = act.astype(o_ref.dtype)
            else:
                packed = _pack_bf16_pair(a, b)
                for c in range(MXU_COLS // LANES):
                    o_ref[pl.ds(c0 // LANES + c, rows, stride=nc_out), :] = (
                        packed[:, c * LANES:(c + 1) * LANES])
        if rows < tile_rows:
            rest = o_ref.shape[0] // tile_rows * rows
            o_ref[rest:, :] = jnp.zeros((o_ref.shape[0] - rest, o_ref.shape[1]), o_ref.dtype)

    valid = valid_ref[i]
    short = MOE_SHORT_ROWS if MOE_SHORT_ROWS < tile_rows else 0

    @pl.when(valid > short)
    def _():
        compute(tile_rows)

    if short:
        @pl.when((valid > 0) & (valid <= short))
        def _():
            compute(short)

    @pl.when(valid == 0)
    def _():
        o_ref[...] = jnp.zeros_like(o_ref)


def _grouped_matmul(tiles, meta, x, w, bias, rows, cols, swiglu):
    n_exp, kdim, n2 = w.shape
    half = n2 // 2
    nj = half // cols
    used = lambda i, meta: jnp.minimum(i, meta[0] - 1)
    if swiglu:
        x_rows = rows * (kdim // 2 // LANES)
        n_slots = x.shape[0] // (kdim // 2 // LANES)
        out_spec = pl.BlockSpec((rows, cols), lambda j, i, te, nx, vl, meta: (i, j))
        out_shape = jax.ShapeDtypeStruct((n_slots, half), BF16)
    else:
        assert nj == 1
        x_rows = rows
        n_slots = x.shape[0]
        out_spec = pl.BlockSpec((rows * (half // LANES), LANES), lambda j, i, te, nx, vl, meta: (i, 0))
        out_shape = jax.ShapeDtypeStruct((n_slots * (half // LANES), LANES), U32)
    n_tiles = n_slots // rows
    return pl.pallas_call(
        functools.partial(_grouped_kernel, swiglu=swiglu),
        grid_spec=pltpu.PrefetchScalarGridSpec(
            num_scalar_prefetch=4,
            grid=(nj, n_tiles),
            in_specs=[pl.BlockSpec((x_rows, x.shape[1]),
                                   lambda j, i, te, nx, vl, meta: (used(i, meta), 0)),
                      pl.BlockSpec(memory_space=pl.ANY),
                      pl.BlockSpec((1, 1, cols), lambda j, i, te, nx, vl, meta: (te[i], 0, j)),
                      pl.BlockSpec((1, 1, cols), lambda j, i, te, nx, vl, meta: (te[i], 0, nj + j))],
            out_specs=out_spec,
            scratch_shapes=[pltpu.VMEM((2, kdim, cols), F32),
                            pltpu.VMEM((kdim, cols), BF16), pltpu.VMEM((kdim, cols), BF16),
                            pltpu.SemaphoreType.DMA((2,))]),
        out_shape=out_shape,
        compiler_params=_params("arbitrary", "arbitrary"),
        name="moe_gate_up" if swiglu else "moe_down",
    )(*tiles, meta, x, w, bias.reshape(n_exp, 1, n2), bias.reshape(n_exp, 1, n2))


def _combine_kernel(dcur_ref, dnext_ref, y_ref, x2_ref, gate_ref, g_ref, o_ref, ybuf, sem, *, rows):
    i = pl.program_id(0)
    n = pl.num_programs(0)

    half = x2_ref.shape[-1] // 2
    nc = half // LANES
    cur = i % 2

    def token(t):
        return pl.ds(pl.multiple_of(t * nc, nc), nc)

    def start_tile(d_ref, slot):
        def step(g, c):
            for j in range(COMBINE_GROUP):
                tok = g * COMBINE_GROUP + j
                for k in range(TOP_K):
                    src = d_ref[0, 0, tok * TOP_K + k]
                    pltpu.make_async_copy(y_ref.at[token(src), :], ybuf.at[slot, k, token(tok), :],
                                          sem.at[slot]).start(priority=k % DMA_PRIORITIES)
            return c

        lax.fori_loop(0, rows // COMBINE_GROUP, step, 0)

    @pl.when(i == 0)
    def _():
        start_tile(dcur_ref, 0)

    @pl.when(i + 1 < n)
    def _():
        start_tile(dnext_ref, 1 - cur)

    for k in range(TOP_K):
        pltpu.make_async_copy(y_ref.at[pl.ds(0, rows * nc), :], ybuf.at[cur, k],
                              sem.at[cur]).wait()

    gate = gate_ref[...]
    acc_lo = x2_ref[:, :half]
    acc_hi = x2_ref[:, half:]
    for k in range(TOP_K):
        lo, hi = _unpack_bf16_pair(
            _load_token_major(lambda s, k=k: ybuf[cur, k, s, :], rows, nc))
        gk = gate[:, k:k + 1]
        acc_lo = acc_lo + gk * lo
        acc_hi = acc_hi + gk * hi
    ms = (jnp.sum(acc_lo * acc_lo, axis=-1, keepdims=True)
          + jnp.sum(acc_hi * acc_hi, axis=-1, keepdims=True)) / (2 * half)
    inv = lax.rsqrt(ms + EPS)
    o_ref[:, :half] = acc_lo * inv * g_ref[:, :half]
    o_ref[:, half:] = acc_hi * inv * g_ref[:, half:]


def _combine(dest, y, x2, gate, g, rows):
    n, d = x2.shape
    nt = n // rows
    return pl.pallas_call(
        functools.partial(_combine_kernel, rows=rows),
        grid=(nt,),
        in_specs=[pl.BlockSpec((1, 1, rows * TOP_K), lambda i: (i, 0, 0), memory_space=pltpu.SMEM),
                  pl.BlockSpec((1, 1, rows * TOP_K), lambda i: (jnp.minimum(i + 1, nt - 1), 0, 0),
                               memory_space=pltpu.SMEM),
                  pl.BlockSpec(memory_space=pl.ANY),
                  pl.BlockSpec((rows, d), lambda i: (i, 0)),
                  pl.BlockSpec((rows, LANES), lambda i: (i, 0)),
                  pl.BlockSpec((1, d), lambda i: (0, 0))],
        out_specs=pl.BlockSpec((rows, d), lambda i: (i, 0)),
        out_shape=jax.ShapeDtypeStruct((n, d), F32),
        scratch_shapes=[pltpu.VMEM((2, TOP_K, rows * (d // 2 // LANES), LANES), U32),
                        pltpu.SemaphoreType.DMA((2,))],
        compiler_params=_params("arbitrary"),
        name="combine_norm",
    )(dest.reshape(nt, 1, rows * TOP_K), dest.reshape(nt, 1, rows * TOP_K), y, x2, gate,
      g.reshape(1, d))


def kernel(x, mem, norm_mix_g, w_in, rel_table, conv_w, conv_b, w_ga, b_ga, w_gx, b_gx, lru_lambda, norm_att_out_g, norm_lru_out_g, w_out, norm_cross_g, norm_mem_g, w_cq, w_ckv, w_co, norm_ffn_g, w_router, b_router, w_gu, b_gu, w_down, b_down, norm_final_g):
    b, seq, d = x.shape
    depth = w_in.shape[0]
    att_width = d // 2
    lru_width = d - att_width
    n_experts = w_router.shape[-1]
    n_tok = b * seq
    mem_len = mem.shape[1]
    assert depth == 1 and n_experts <= LANES and seq % min(ATT_QBLOCK, seq) == 0

    moe_rows = min(MOE_ROWS, n_tok)
    n_tiles = (n_tok * TOP_K + n_experts * (moe_rows - 1)) // moe_rows
    n_slots = n_tiles * moe_rows

    for l in range(depth):
        qkv, xy = _norm_proj(x.reshape(n_tok, d), norm_mix_g[l], w_in[l].astype(BF16),
                             [(3 * att_width, BF16), (2 * lru_width, F32)], min(PROJ_ROWS, seq))
        att = _attention(qkv.reshape(b, seq, 3 * att_width), rel_table[l], att_width)
        lru = _lru(xy.reshape(b, seq, 2 * lru_width), conv_w[l], conv_b[l], w_ga[l], b_ga[l],
                   w_gx[l], b_gx[l], lru_lambda[l], lru_width)
        (ckv,) = _norm_proj(mem.reshape(b * mem_len, d), norm_mem_g[l], w_ckv[l].astype(BF16),
                            [(w_ckv.shape[-1], BF16)], min(PROJ_ROWS, mem_len))
        ckv = ckv.reshape(b, mem_len, -1)
        wr_pad = jnp.zeros((d, LANES), F32).at[:, :n_experts].set(w_router[l]).astype(BF16)
        br_pad = jnp.full((1, LANES), NEG, F32).at[0, :n_experts].set(b_router[l])
        x2, hpack, route, gate, counts = _mid(
            x, att, lru, ckv, norm_att_out_g[l], norm_lru_out_g[l], w_out[l].astype(BF16),
            norm_cross_g[l], w_cq[l].astype(BF16), w_co[l].astype(BF16), norm_ffn_g[l],
            wr_pad, br_pad, n_experts, min(MID_ROWS, seq))
        counts = counts[0, :n_experts].astype(I32)
        padded = (counts + moe_rows - 1) // moe_rows * moe_rows
        pad_end = jnp.cumsum(padded)
        pad_start = pad_end - padded
        expert_ids = jnp.arange(n_experts, dtype=I32)
        sel_e = route[:, :TOP_K]
        start_of = jnp.sum(jnp.where(sel_e[:, :, None] == expert_ids, pad_start, 0), axis=-1)
        dest = start_of + route[:, TOP_K:2 * TOP_K]
        total = pad_end[-1]
        meta = (total // moe_rows).astype(I32).reshape(1)
        tile_start = jnp.minimum(jnp.arange(n_tiles, dtype=I32) * moe_rows, total - moe_rows)
        tile_expert = jnp.sum(tile_start[:, None] >= pad_end[None, :], axis=1).astype(I32)
        later = (expert_ids[None, :] > expert_ids[:, None]) & (padded[None, :] > 0)
        next_expert = jnp.min(jnp.where(later, expert_ids[None, :], n_experts), axis=1)
        next_expert = jnp.where(next_expert == n_experts, -1, next_expert)
        of_tile = lambda table: jnp.sum(
            jnp.where(tile_expert[:, None] == expert_ids, table, 0), axis=1).astype(I32)
        tile_next = of_tile(next_expert)
        in_use = jnp.arange(n_tiles, dtype=I32) * moe_rows < total
        tile_valid = jnp.where(in_use, jnp.clip(
            of_tile(pad_start + counts) - tile_start, 0, moe_rows), 0).astype(I32)
        tiles = (tile_expert, tile_next, tile_valid)
        fill = jnp.concatenate([pad_start + counts, total[None], padded - counts,
                                ((n_slots - total) // (moe_rows // 2))[None]]).astype(I32)
        xs = _dispatch(fill, dest, hpack, n_slots, min(DISPATCH_ROWS, n_tok), moe_rows,
                       d // 2 // LANES)
        cols = min(MOE_COLS, w_down.shape[2] // 2)
        act = _grouped_matmul(tiles, meta, xs, w_gu[l], b_gu[l], moe_rows, cols, True)
        y = _grouped_matmul(tiles, meta, act, w_down[l], b_down[l], moe_rows,
                            min(MOE_COLS, d // 2), False)
        x = _combine(dest, y, x2, gate, norm_final_g, min(COMBINE_ROWS, n_tok)).reshape(b, seq, d)
    return x
```

```python
import functools

import jax
import jax.numpy as jnp
from jax import lax
from jax.experimental import pallas as pl
from jax.experimental.pallas import tpu as pltpu

F32 = jnp.float32
BF16 = jnp.bfloat16
U32 = jnp.uint32
I32 = jnp.int32

EPS = 1e-6
CHUNK = 64
LEFT_CHUNKS = 8
ATT_HEAD_DIM = 64
MAX_REL = 2 * CHUNK
CONV_WIDTH = 4
LRU_C = 8.0
X_HEADS = 4
TOP_K = 4
SWIGLU_ALPHA = 1.702
SWIGLU_LIMIT = 7.0
NEG = -1e30
TINY = 1e-37

LANES = 128
SUBLANES = 8
MXU_COLS = 256
DMA_PRIORITIES = 2
VMEM_LIMIT = 56 * 1024 * 1024

PROJ_ROWS = 256
ATT_QBLOCK = 256
MID_ROWS = 512
MOE_ROWS = 1024
MOE_COLS = 1024
CAST_ROWS = 64
MOE_PART_ROWS = (128, 512)
DISPATCH_ROWS = 256
DISPATCH_GROUP = 8
COMBINE_ROWS = 256
COMBINE_GROUP = 8


def _rms(x, g):
    ms = jnp.mean(x * x, axis=-1, keepdims=True)
    return x * lax.rsqrt(ms + EPS) * g


def _fold_lanes(x, op):
    acc = x[:, :LANES]
    for c in range(LANES, x.shape[-1], LANES):
        acc = op(acc, x[:, c:c + LANES])
    return acc


def _params(*sem):
    return pltpu.CompilerParams(dimension_semantics=sem, vmem_limit_bytes=VMEM_LIMIT)


def _norm_proj_kernel(x_ref, g_ref, w_ref, *out_refs, col_chunk):
    h = _rms(x_ref[...], g_ref[...]).astype(BF16)
    c0 = 0
    for o_ref in out_refs:
        n = o_ref.shape[-1]
        for s in range(0, n, col_chunk):
            e = min(s + col_chunk, n)
            o_ref[:, s:e] = jnp.dot(h, w_ref[:, c0 + s:c0 + e],
                                    preferred_element_type=F32).astype(o_ref.dtype)
        c0 += n


def _norm_proj(x2d, g, w_bf, outs, rows):
    m, d = x2d.shape
    n_total = w_bf.shape[1]
    assert sum(n for n, _ in outs) == n_total and m % rows == 0
    return pl.pallas_call(
        functools.partial(_norm_proj_kernel, col_chunk=512),
        grid=(m // rows,),
        in_specs=[pl.BlockSpec((rows, d), lambda i: (i, 0)),
                  pl.BlockSpec((1, d), lambda i: (0, 0)),
                  pl.BlockSpec((d, n_total), lambda i: (0, 0), pipeline_mode=pl.Buffered(1))],
        out_specs=[pl.BlockSpec((rows, n), lambda i: (i, 0)) for n, _ in outs],
        out_shape=[jax.ShapeDtypeStruct((m, n), dt) for n, dt in outs],
        compiler_params=_params("parallel"),
        name="norm_proj",
    )(x2d, g.reshape(1, d), w_bf)


def _attn_kernel(q_ref, k_ref, v_ref, bias_ref, o_ref, kpad, vpad, *, seq, pad, qb, kb):
    hd = ATT_HEAD_DIM
    kpad[0:pad, :] = jnp.zeros((pad, LANES), BF16)
    vpad[0:pad, :] = jnp.zeros((pad, LANES), BF16)
    kpad[pad:pad + seq, :] = k_ref[0]
    vpad[pad:pad + seq, :] = v_ref[0]
    scale = hd ** -0.5
    kcol = lax.broadcasted_iota(I32, (qb, kb), 1)
    head_of_lane = lax.broadcasted_iota(I32, (qb, LANES), 1) // hd

    def block(ib, carry, *, near_start):
        s0 = pl.multiple_of(ib * qb, qb)
        q = q_ref[0, pl.ds(s0, qb), :] * scale
        kblk = kpad[pl.ds(s0, kb), :]
        vblk = vpad[pl.ds(s0, kb), :]
        out = None
        for hh in range(LANES // hd):
            mine = head_of_lane == hh
            s = lax.dot_general(jnp.where(mine, q, 0.0), kblk, (((1,), (1,)), ((), ())),
                                preferred_element_type=F32)
            s = s + bias_ref[hh]
            if near_start:
                s = jnp.where(kcol < (pad - s0), NEG, s)
            m = jnp.max(_fold_lanes(s, jnp.maximum), axis=-1, keepdims=True)
            p = jnp.exp(s - m)
            l = jnp.sum(_fold_lanes(p, jnp.add), axis=-1, keepdims=True)
            o = jnp.dot(p.astype(BF16), vblk, preferred_element_type=F32) / l
            out = o if out is None else jnp.where(mine, o, out)
        o_ref[0, pl.ds(s0, qb), :] = out.astype(o_ref.dtype)
        return carry

    n_near = min(pad // qb, seq // qb)
    lax.fori_loop(0, n_near, functools.partial(block, near_start=True), 0, unroll=2)
    lax.fori_loop(n_near, seq // qb, functools.partial(block, near_start=False), 0, unroll=3)


def _attention(qkv, rel_table, att_width):
    b, seq, _ = qkv.shape
    heads = att_width // ATT_HEAD_DIM
    hp = LANES // ATT_HEAD_DIM
    qb = min(ATT_QBLOCK, seq)
    pad = LEFT_CHUNKS * CHUNK
    kb = qb + pad
    ql = jnp.arange(qb)[:, None]
    kl = jnp.arange(kb)[None, :]
    span = qb + kb - 1
    rel_u = (qb - 1 + pad) - jnp.arange(span)
    u = rel_table[:, jnp.clip(rel_u, -MAX_REL, MAX_REL) + MAX_REL].astype(F32)
    u = jnp.concatenate([u, jnp.zeros((heads, 1), F32)], axis=1)
    skew = jnp.tile(u, (1, qb))[:, :qb * span].reshape(heads, qb, span)
    bias = skew[:, :, qb - 1:qb - 1 + kb]
    dchunk = ql // CHUNK + LEFT_CHUNKS - kl // CHUNK
    band = (dchunk >= 0) & (dchunk <= LEFT_CHUNKS)
    bias = jnp.where(band[None], bias, NEG)
    nblk = att_width // LANES
    return pl.pallas_call(
        functools.partial(_attn_kernel, seq=seq, pad=pad, qb=qb, kb=kb),
        grid=(b, heads // hp),
        in_specs=[pl.BlockSpec((1, seq, LANES), lambda i, j: (i, 0, j)),
                  pl.BlockSpec((1, seq, LANES), lambda i, j: (i, 0, nblk + j)),
                  pl.BlockSpec((1, seq, LANES), lambda i, j: (i, 0, 2 * nblk + j)),
                  pl.BlockSpec((hp, qb, kb), lambda i, j: (j, 0, 0))],
        out_specs=pl.BlockSpec((1, seq, LANES), lambda i, j: (i, 0, j)),
        out_shape=jax.ShapeDtypeStruct((b, seq, att_width), BF16),
        scratch_shapes=[pltpu.VMEM((seq + pad, LANES), BF16),
                        pltpu.VMEM((seq + pad, LANES), BF16)],
        compiler_params=_params("parallel", "parallel"),
        name="band_attention",
    )(qkv, qkv, qkv, bias)


def _lru_kernel(xr_ref, yg_ref, cw_ref, cb_ref, wg_ref, bg_ref, lam_ref, o_ref,
                a_scr, b_scr, h_scr, *, seq):
    bw = xr_ref.shape[-1]
    x = xr_ref[0]
    row = lax.broadcasted_iota(I32, (seq, bw), 0)
    cw = cw_ref[...]
    xc = cb_ref[...] + cw[CONV_WIDTH - 1:CONV_WIDTH, :] * x
    for j in range(1, CONV_WIDTH):
        xs = jnp.where(row >= j, pltpu.roll(x, j, 0), 0.0)
        xc = xc + cw[CONV_WIDTH - 1 - j:CONV_WIDTH - j, :] * xs
    gates = jnp.dot(xc.astype(BF16), wg_ref[0], preferred_element_type=F32) + bg_ref[0]
    r = 1.0 / (1.0 + jnp.exp(-gates[:, :bw]))
    i = 1.0 / (1.0 + jnp.exp(-gates[:, bw:]))
    z = -lam_ref[...]
    softplus = jnp.maximum(z, 0.0) + jnp.log1p(jnp.exp(-jnp.abs(z)))
    a = jnp.exp(-LRU_C * r * softplus)
    v = 1.0 - a * a
    b = v * lax.rsqrt(jnp.maximum(v, TINY)) * (i * xc)
    a_scr[...] = a
    b_scr[...] = b
    ng = seq // SUBLANES
    ga, gb = [], []
    for j in range(SUBLANES):
        aj = a_scr[pl.ds(j, ng, stride=SUBLANES), :]
        bj = b_scr[pl.ds(j, ng, stride=SUBLANES), :]
        if j:
            bj = aj * gb[-1] + bj
            aj = aj * ga[-1]
        ga.append(aj)
        gb.append(bj)
    ta, tb = ga[-1], gb[-1]
    grow = lax.broadcasted_iota(I32, (ng, bw), 0)
    k = 1
    while k < ng:
        keep = grow >= k
        tb = jnp.where(keep, ta * pltpu.roll(tb, k, 0) + tb, tb)
        ta = jnp.where(keep, ta * pltpu.roll(ta, k, 0), ta)
        k *= 2
    h_in = jnp.where(grow >= 1, pltpu.roll(tb, 1, 0), 0.0)
    for j in range(SUBLANES):
        h_scr[pl.ds(j, ng, stride=SUBLANES), :] = ga[j] * h_in + gb[j]
    g = yg_ref[0]
    gelu = 0.5 * g * (1.0 + jnp.tanh(0.7978845608028654 * (g + 0.044715 * (g * g * g))))
    o_ref[0] = (h_scr[...] * gelu).astype(o_ref.dtype)


def _lru(xy, conv_w, conv_b, w_ga, b_ga, w_gx, b_gx, lam, lru_width):
    b, seq, _ = xy.shape
    nb, bw, _ = w_ga.shape
    wg = jnp.concatenate([w_ga, w_gx], axis=-1).astype(BF16)
    bg = jnp.concatenate([b_ga, b_gx], axis=-1).reshape(nb, 1, 2 * bw)
    return pl.pallas_call(
        functools.partial(_lru_kernel, seq=seq),
        grid=(b, nb),
        in_specs=[pl.BlockSpec((1, seq, bw), lambda i, n: (i, 0, n)),
                  pl.BlockSpec((1, seq, bw), lambda i, n: (i, 0, nb + n)),
                  pl.BlockSpec((CONV_WIDTH, bw), lambda i, n: (0, n)),
                  pl.BlockSpec((1, bw), lambda i, n: (0, n)),
                  pl.BlockSpec((1, bw, 2 * bw), lambda i, n: (n, 0, 0)),
                  pl.BlockSpec((1, 1, 2 * bw), lambda i, n: (n, 0, 0)),
                  pl.BlockSpec((1, bw), lambda i, n: (0, n))],
        out_specs=pl.BlockSpec((1, seq, bw), lambda i, n: (i, 0, n)),
        out_shape=jax.ShapeDtypeStruct((b, seq, lru_width), BF16),
        scratch_shapes=[pltpu.VMEM((seq, bw), F32)] * 3,
        compiler_params=_params("parallel", "parallel"),
        name="rg_lru",
    )(xy, xy, conv_w, conv_b.reshape(1, -1), wg, bg, lam.reshape(1, -1))


def _pack_bf16_pair(lo, hi):
    lo_b = pltpu.bitcast(lo.astype(BF16).astype(F32), U32) >> 16
    hi_b = pltpu.bitcast(hi.astype(BF16).astype(F32), U32) & jnp.uint32(0xFFFF0000)
    return hi_b | lo_b


def _unpack_bf16_pair(w):
    lo = pltpu.bitcast(w << 16, F32)
    hi = pltpu.bitcast(w & jnp.uint32(0xFFFF0000), F32)
    return lo, hi


def _store_token_major(ref, x, nc=None):
    tokens = x.shape[0]
    nc = nc or x.shape[-1] // LANES
    for c in range(x.shape[-1] // LANES):
        ref[pl.ds(c, tokens, stride=nc), :] = x[:, c * LANES:(c + 1) * LANES]


def _load_token_major(load, tokens, nc):
    return jnp.concatenate([load(pl.ds(c, tokens, stride=nc)) for c in range(nc)], axis=-1)


def _mid_kernel(x_ref, att_ref, lru_ref, ck_ref, cv_ref, ga_ref, gl_ref, wo_ref, gc_ref, wcq_ref,
                wco_ref, gf_ref, wr_ref, br_ref,
                x2_ref, hp_ref, route_ref, gate_ref, cnt_ref, cnt_scr, *, n_experts):
    rows, d = x_ref.shape
    aw = att_ref.shape[-1]

    @pl.when((pl.program_id(0) == 0) & (pl.program_id(1) == 0))
    def _():
        cnt_scr[...] = jnp.zeros_like(cnt_scr)

    att_n = _rms(att_ref[...].astype(F32), ga_ref[...]).astype(BF16)
    lru_n = _rms(lru_ref[...].astype(F32), gl_ref[...]).astype(BF16)
    x1 = (x_ref[...]
          + jnp.dot(att_n, wo_ref[0:aw, :], preferred_element_type=F32)
          + jnp.dot(lru_n, wo_ref[aw:, :], preferred_element_type=F32))

    hq = _rms(x1, gc_ref[...]).astype(BF16)
    xw = wcq_ref.shape[-1]
    xhd = xw // X_HEADS
    cq = (jnp.dot(hq, wcq_ref[...], preferred_element_type=F32) * (xhd ** -0.5)).astype(BF16)
    ck = ck_ref[0]
    cv = cv_ref[0]
    heads = []
    for h in range(X_HEADS):
        sl = slice(h * xhd, (h + 1) * xhd)
        s = lax.dot_general(cq[:, sl], ck[:, sl], (((1,), (1,)), ((), ())),
                            preferred_element_type=F32)
        m = jnp.max(s, axis=-1, keepdims=True)
        p = jnp.exp(s - m)
        l = jnp.sum(p, axis=-1, keepdims=True)
        heads.append(jnp.dot(p.astype(BF16), cv[:, sl], preferred_element_type=F32) / l)
    o = jnp.concatenate(heads, axis=-1).astype(BF16)
    x2 = x1 + jnp.dot(o, wco_ref[...], preferred_element_type=F32)
    x2_ref[...] = x2

    hn = _rms(x2, gf_ref[...])
    _store_token_major(hp_ref, _pack_bf16_pair(hn[:, :d // 2], hn[:, d // 2:]))

    logits = jnp.dot(hn.astype(BF16), wr_ref[...], preferred_element_type=F32) + br_ref[...]
    lane = lax.broadcasted_iota(I32, (rows, LANES), 1).astype(F32)
    work = logits
    sel_e, sel_v = [], []
    onehot = jnp.zeros((rows, LANES), F32)
    for _ in range(TOP_K):
        v = jnp.max(work, axis=-1, keepdims=True)
        e = jnp.min(jnp.where(work == v, lane, float(LANES)), axis=-1, keepdims=True)
        hit = lane == e
        onehot = jnp.where(hit, 1.0, onehot)
        work = jnp.where(hit, NEG * 2, work)
        sel_e.append(e)
        sel_v.append(v)
    ex = [jnp.exp(v - sel_v[0]) for v in sel_v]
    den = ex[0] + ex[1] + ex[2] + ex[3]

    ri = lax.broadcasted_iota(I32, (rows, rows), 0)
    ci = lax.broadcasted_iota(I32, (rows, rows), 1)
    tri = jnp.where(ci < ri, 1.0, 0.0).astype(BF16)
    before = jnp.dot(tri, onehot.astype(BF16), preferred_element_type=F32) + cnt_scr[...]
    route = jnp.zeros((rows, LANES), F32)
    gate = jnp.zeros((rows, LANES), F32)
    for k in range(TOP_K):
        rank = jnp.sum(jnp.where(lane == sel_e[k], before, 0.0), axis=-1, keepdims=True)
        route = jnp.where(lane == float(k), sel_e[k], route)
        route = jnp.where(lane == float(TOP_K + k), rank, route)
        gate = jnp.where(lane == float(k), ex[k] / den, gate)
    route_ref[...] = route.astype(I32)
    gate_ref[...] = gate
    cnt_scr[...] = cnt_scr[...] + jnp.sum(onehot, axis=0, keepdims=True)
    cnt_ref[...] = cnt_scr[...]


def _mid(x, att, lru, ckv, ga, gl, wo_bf, gc, wcq_bf, wco_bf, gf, wr_pad, br_pad, n_experts, rows):
    b, seq, d = x.shape
    aw, lw = att.shape[-1], lru.shape[-1]
    mem_len, xw2 = ckv.shape[1], ckv.shape[2]
    xw = xw2 // 2
    n = b * seq
    nt = seq // rows
    row_map = lambda i, t: (i * nt + t, 0)
    const = lambda i, t: (0, 0)
    res = lambda shape: pl.BlockSpec(shape, const, pipeline_mode=pl.Buffered(1))
    return pl.pallas_call(
        functools.partial(_mid_kernel, n_experts=n_experts),
        grid=(b, nt),
        in_specs=[pl.BlockSpec((rows, d), row_map),
                  pl.BlockSpec((rows, aw), row_map),
                  pl.BlockSpec((rows, lw), row_map),
                  pl.BlockSpec((1, mem_len, xw), lambda i, t: (i, 0, 0)),
                  pl.BlockSpec((1, mem_len, xw), lambda i, t: (i, 0, 1)),
                  res((1, aw)), res((1, lw)), res((aw + lw, d)), res((1, d)), res((d, xw)),
                  res((xw, d)), res((1, d)), res((d, LANES)), res((1, LANES))],
        out_specs=[pl.BlockSpec((rows, d), row_map),
                   pl.BlockSpec((rows * (d // 2 // LANES), LANES), row_map),
                   pl.BlockSpec((rows, LANES), row_map),
                   pl.BlockSpec((rows, LANES), row_map),
                   pl.BlockSpec((1, LANES), const)],
        out_shape=[jax.ShapeDtypeStruct((n, d), F32),
                   jax.ShapeDtypeStruct((n * (d // 2 // LANES), LANES), U32),
                   jax.ShapeDtypeStruct((n, LANES), I32),
                   jax.ShapeDtypeStruct((n, LANES), F32),
                   jax.ShapeDtypeStruct((1, LANES), F32)],
        scratch_shapes=[pltpu.VMEM((1, LANES), F32)],
        compiler_params=_params("arbitrary", "arbitrary"),
        name="mix_cross_router",
    )(x.reshape(n, d), att.reshape(n, aw), lru.reshape(n, lw), ckv, ckv,
      ga.reshape(1, aw), gl.reshape(1, lw), wo_bf, gc.reshape(1, d), wcq_bf, wco_bf,
      gf.reshape(1, d), wr_pad, br_pad)


def _dispatch_kernel(fill_ref, dest_ref, src_ref, xs_ref, buf, zbuf, lsem, ssem, zsem,
                     *, rows, nc, n_fill, tail_chunk):
    i = pl.program_id(0)
    n = pl.num_programs(0)
    ztok = zbuf.shape[0] // nc

    def tokens(first, count):
        return pl.ds(pl.multiple_of(first * nc, nc), count * nc)

    def load(t, slot):
        return pltpu.make_async_copy(src_ref.at[tokens(t * rows, rows), :], buf.at[slot],
                                     lsem.at[slot])

    def zero_copy(start, size):
        return pltpu.make_async_copy(zbuf.at[pl.ds(0, size * nc), :],
                                     xs_ref.at[tokens(start, size), :], zsem)

    def fill_chunks(fn):
        for e in range(n_fill):
            start = fill_ref[e]
            length = fill_ref[n_fill + 1 + e]
            size = ztok
            while size >= 1:
                part = length & size
                @pl.when(part != 0)
                def _(start=start, size=size):
                    fn(zero_copy(start, size))
                start = start + part
                size //= 2
        tail_start = fill_ref[n_fill]
        n_tail = fill_ref[2 * n_fill + 1]

        def tail(c, carry):
            fn(zero_copy(tail_start + c * tail_chunk, tail_chunk))
            return carry

        lax.fori_loop(0, n_tail, tail, 0)

    @pl.when(i == 0)
    def _():
        load(0, 0).start()
        zbuf[...] = jnp.zeros_like(zbuf)
        fill_chunks(lambda c: c.start())
        fill_chunks(lambda c: c.wait())

    @pl.when(i + 1 < n)
    def _():
        load(i + 1, (i + 1) % 3).start()

    slot = i % 3
    par = i % 2
    load(i, slot).wait()

    def start_rows(g, c):
        for j in range(DISPATCH_GROUP):
            tok = g * DISPATCH_GROUP + j
            for k in range(TOP_K):
                dst = dest_ref[0, 0, tok * TOP_K + k]
                pltpu.make_async_copy(buf.at[slot, tokens(tok, 1), :],
                                      xs_ref.at[tokens(dst, 1), :],
                                      ssem.at[par]).start(priority=k % DMA_PRIORITIES)
        return c

    lax.fori_loop(0, rows // DISPATCH_GROUP, start_rows, 0)

    def wait_tile(p):
        for _ in range(TOP_K):
            pltpu.make_async_copy(buf.at[0], xs_ref.at[pl.ds(0, rows * nc), :], ssem.at[p]).wait()

    @pl.when(i > 0)
    def _():
        wait_tile(1 - par)

    @pl.when(i == n - 1)
    def _():
        wait_tile(par)


def _dispatch(fill, dest, src, n_slots, rows, moe_rows, nc):
    n_tok = src.shape[0] // nc
    nt = n_tok // rows
    n_fill = (fill.shape[0] - 2) // 2
    ztok = moe_rows // 2
    return pl.pallas_call(
        functools.partial(_dispatch_kernel, rows=rows, nc=nc, n_fill=n_fill, tail_chunk=ztok),
        grid_spec=pltpu.PrefetchScalarGridSpec(
            num_scalar_prefetch=1,
            grid=(nt,),
            in_specs=[pl.BlockSpec((1, 1, rows * TOP_K), lambda i, fill: (i, 0, 0),
                                   memory_space=pltpu.SMEM),
                      pl.BlockSpec(memory_space=pl.ANY)],
            out_specs=pl.BlockSpec(memory_space=pl.ANY),
            scratch_shapes=[pltpu.VMEM((3, rows * nc, LANES), src.dtype),
                            pltpu.VMEM((ztok * nc, LANES), src.dtype),
                            pltpu.SemaphoreType.DMA((3,)),
                            pltpu.SemaphoreType.DMA((2,)),
                            pltpu.SemaphoreType.DMA(())]),
        out_shape=jax.ShapeDtypeStruct((n_slots * nc, LANES), src.dtype),
        compiler_params=_params("arbitrary"),
        name="dispatch_scatter",
    )(fill, dest.reshape(nt, 1, rows * TOP_K), src)


def _grouped_kernel(te_ref, nxt_ref, valid_ref, meta_ref, x_ref, w_ref, ba_ref, bb_ref, o_ref,
                    w_f32, wa_bf, wb_bf, wsem, *, swiglu):
    j = pl.program_id(0)
    i = pl.program_id(1)
    nj = pl.num_programs(0)
    cols = wa_bf.shape[-1]
    expert = te_ref[i]
    new_expert = (i == 0) | (expert != te_ref[jnp.maximum(i - 1, 0)])

    def weight_copy(jj, ee, which):
        col = pl.multiple_of((which * nj + jj) * cols, cols)
        return pltpu.make_async_copy(w_ref.at[ee, :, pl.ds(col, cols)], w_f32.at[which],
                                     wsem.at[which])

    def start_weights(jj, ee):
        weight_copy(jj, ee, 0).start()
        weight_copy(jj, ee, 1).start()

    @pl.when((j == 0) & (i == 0))
    def _():
        start_weights(0, expert)

    @pl.when(new_expert)
    def _():
        weight_copy(j, expert, 0).wait()
        weight_copy(j, expert, 1).wait()
        def cast_rows(c, carry):
            rs = pl.ds(pl.multiple_of(c * CAST_ROWS, CAST_ROWS), CAST_ROWS)
            wa_bf[rs, :] = w_f32[0, rs, :].astype(BF16)
            wb_bf[rs, :] = w_f32[1, rs, :].astype(BF16)
            return carry

        lax.fori_loop(0, wa_bf.shape[0] // CAST_ROWS, cast_rows, 0)
        following = nxt_ref[i]

        @pl.when(following >= 0)
        def _():
            start_weights(j, following)

        @pl.when((following < 0) & (j + 1 < nj))
        def _():
            start_weights(j + 1, te_ref[0])

    tile_rows = o_ref.shape[0] if swiglu else x_ref.shape[0]
    nc_in = x_ref.shape[0] // tile_rows
    nc_out = o_ref.shape[0] // tile_rows

    def compute(rows):
        if swiglu:
            lo, hi = _unpack_bf16_pair(_load_token_major(lambda s: x_ref[s, :], rows, nc_in))
            xb = jnp.concatenate([lo.astype(BF16), hi.astype(BF16)], axis=-1)
        else:
            xb = x_ref[0:rows, :]
        for c0 in range(0, cols, MXU_COLS):
            cs = slice(c0, c0 + MXU_COLS)
            a = jnp.dot(xb, wa_bf[:, cs], preferred_element_type=F32) + ba_ref[0, :, cs]
            b = jnp.dot(xb, wb_bf[:, cs], preferred_element_type=F32) + bb_ref[0, :, cs]
            if swiglu:
                gate = jnp.minimum(a, SWIGLU_LIMIT)
                up = jnp.clip(b, -SWIGLU_LIMIT, SWIGLU_LIMIT)
                act = (up + 1.0) * gate * (1.0 / (1.0 + jnp.exp(-SWIGLU_ALPHA * gate)))
                o_ref[0:rows, cs] = act.astype(o_ref.dtype)
            else:
                packed = _pack_bf16_pair(a, b)
                for c in range(MXU_COLS // LANES):
                    o_ref[pl.ds(c0 // LANES + c, rows, stride=nc_out), :] = (
                        packed[:, c * LANES:(c + 1) * LANES])
        if rows < tile_rows:
            rest = o_ref.shape[0] // tile_rows * rows
            o_ref[rest:, :] = jnp.zeros((o_ref.shape[0] - rest, o_ref.shape[1]), o_ref.dtype)

    valid = valid_ref[i]
    below = 0
    for part in [p for p in MOE_PART_ROWS if p < tile_rows] + [tile_rows]:
        @pl.when((valid > below) & (valid <= part))
        def _(part=part):
            compute(part)
        below = part

    @pl.when(valid == 0)
    def _():
        o_ref[...] = jnp.zeros_like(o_ref)


def _grouped_matmul(tiles, meta, x, w, bias, rows, cols, swiglu):
    n_exp, kdim, n2 = w.shape
    half = n2 // 2
    nj = half // cols
    used = lambda i, meta: jnp.minimum(i, meta[0] - 1)
    if swiglu:
        x_rows = rows * (kdim // 2 // LANES)
        n_slots = x.shape[0] // (kdim // 2 // LANES)
        out_spec = pl.BlockSpec((rows, cols), lambda j, i, te, nx, vl, meta: (i, j))
        out_shape = jax.ShapeDtypeStruct((n_slots, half), BF16)
    else:
        assert nj == 1
        x_rows = rows
        n_slots = x.shape[0]
        out_spec = pl.BlockSpec((rows * (half // LANES), LANES), lambda j, i, te, nx, vl, meta: (i, 0))
        out_shape = jax.ShapeDtypeStruct((n_slots * (half // LANES), LANES), U32)
    n_tiles = n_slots // rows
    return pl.pallas_call(
        functools.partial(_grouped_kernel, swiglu=swiglu),
        grid_spec=pltpu.PrefetchScalarGridSpec(
            num_scalar_prefetch=4,
            grid=(nj, n_tiles),
            in_specs=[pl.BlockSpec((x_rows, x.shape[1]),
                                   lambda j, i, te, nx, vl, meta: (used(i, meta), 0)),
                      pl.BlockSpec(memory_space=pl.ANY),
                      pl.BlockSpec((1, 1, cols), lambda j, i, te, nx, vl, meta: (te[i], 0, j)),
                      pl.BlockSpec((1, 1, cols), lambda j, i, te, nx, vl, meta: (te[i], 0, nj + j))],
            out_specs=out_spec,
            scratch_shapes=[pltpu.VMEM((2, kdim, cols), F32),
                            pltpu.VMEM((kdim, cols), BF16), pltpu.VMEM((kdim, cols), BF16),
                            pltpu.SemaphoreType.DMA((2,))]),
        out_shape=out_shape,
        compiler_params=_params("arbitrary", "arbitrary"),
        name="moe_gate_up" if swiglu else "moe_down",
    )(*tiles, meta, x, w, bias.reshape(n_exp, 1, n2), bias.reshape(n_exp, 1, n2))


def _combine_kernel(dcur_ref, dnext_ref, y_ref, x2_ref, gate_ref, g_ref, o_ref, ybuf, sem, *, rows):
    i = pl.program_id(0)
    n = pl.num_programs(0)

    half = x2_ref.shape[-1] // 2
    nc = half // LANES
    cur = i % 2

    def token(t):
        return pl.ds(pl.multiple_of(t * nc, nc), nc)

    def start_tile(d_ref, slot):
        def step(g, c):
            for j in range(COMBINE_GROUP):
                tok = g * COMBINE_GROUP + j
                for k in range(TOP_K):
                    src = d_ref[0, 0, tok * TOP_K + k]
                    pltpu.make_async_copy(y_ref.at[token(src), :], ybuf.at[slot, k, token(tok), :],
                                          sem.at[slot]).start(priority=k % DMA_PRIORITIES)
            return c

        lax.fori_loop(0, rows // COMBINE_GROUP, step, 0)

    @pl.when(i == 0)
    def _():
        start_tile(dcur_ref, 0)

    @pl.when(i + 1 < n)
    def _():
        start_tile(dnext_ref, 1 - cur)

    for k in range(TOP_K):
        pltpu.make_async_copy(y_ref.at[pl.ds(0, rows * nc), :], ybuf.at[cur, k],
                              sem.at[cur]).wait()

    gate = gate_ref[...]
    acc_lo = x2_ref[:, :half]
    acc_hi = x2_ref[:, half:]
    for k in range(TOP_K):
        lo, hi = _unpack_bf16_pair(
            _load_token_major(lambda s, k=k: ybuf[cur, k, s, :], rows, nc))
        gk = gate[:, k:k + 1]
        acc_lo = acc_lo + gk * lo
        acc_hi = acc_hi + gk * hi
    ms = (jnp.sum(acc_lo * acc_lo, axis=-1, keepdims=True)
          + jnp.sum(acc_hi * acc_hi, axis=-1, keepdims=True)) / (2 * half)
    inv = lax.rsqrt(ms + EPS)
    o_ref[:, :half] = acc_lo * inv * g_ref[:, :half]
    o_ref[:, half:] = acc_hi * inv * g_ref[:, half:]


def _combine(dest, y, x2, gate, g, rows):
    n, d = x2.shape
    nt = n // rows
    return pl.pallas_call(
        functools.partial(_combine_kernel, rows=rows),
        grid=(nt,),
        in_specs=[pl.BlockSpec((1, 1, rows * TOP_K), lambda i: (i, 0, 0), memory_space=pltpu.SMEM),
                  pl.BlockSpec((1, 1, rows * TOP_K), lambda i: (jnp.minimum(i + 1, nt - 1), 0, 0),
                               memory_space=pltpu.SMEM),
                  pl.BlockSpec(memory_space=pl.ANY),
                  pl.BlockSpec((rows, d), lambda i: (i, 0)),
                  pl.BlockSpec((rows, LANES), lambda i: (i, 0)),
                  pl.BlockSpec((1, d), lambda i: (0, 0))],
        out_specs=pl.BlockSpec((rows, d), lambda i: (i, 0)),
        out_shape=jax.ShapeDtypeStruct((n, d), F32),
        scratch_shapes=[pltpu.VMEM((2, TOP_K, rows * (d // 2 // LANES), LANES), U32),
                        pltpu.SemaphoreType.DMA((2,))],
        compiler_params=_params("arbitrary"),
        name="combine_norm",
    )(dest.reshape(nt, 1, rows * TOP_K), dest.reshape(nt, 1, rows * TOP_K), y, x2, gate,
      g.reshape(1, d))


def kernel(x, mem, norm_mix_g, w_in, rel_table, conv_w, conv_b, w_ga, b_ga, w_gx, b_gx, lru_lambda, norm_att_out_g, norm_lru_out_g, w_out, norm_cross_g, norm_mem_g, w_cq, w_ckv, w_co, norm_ffn_g, w_router, b_router, w_gu, b_gu, w_down, b_down, norm_final_g):
    b, seq, d = x.shape
    depth = w_in.shape[0]
    att_width = d // 2
    lru_width = d - att_width
    n_experts = w_router.shape[-1]
    n_tok = b * seq
    mem_len = mem.shape[1]
    assert depth == 1 and n_experts <= LANES and seq % min(ATT_QBLOCK, seq) == 0

    moe_rows = min(MOE_ROWS, n_tok)
    n_tiles = (n_tok * TOP_K + n_experts * (moe_rows - 1)) // moe_rows
    n_slots = n_tiles * moe_rows

    for l in range(depth):
        qkv, xy = _norm_proj(x.reshape(n_tok, d), norm_mix_g[l], w_in[l].astype(BF16),
                             [(3 * att_width, BF16), (2 * lru_width, F32)], min(PROJ_ROWS, seq))
        att = _attention(qkv.reshape(b, seq, 3 * att_width), rel_table[l], att_width)
        lru = _lru(xy.reshape(b, seq, 2 * lru_width), conv_w[l], conv_b[l], w_ga[l], b_ga[l],
                   w_gx[l], b_gx[l], lru_lambda[l], lru_width)
        (ckv,) = _norm_proj(mem.reshape(b * mem_len, d), norm_mem_g[l], w_ckv[l].astype(BF16),
                            [(w_ckv.shape[-1], BF16)], min(PROJ_ROWS, mem_len))
        ckv = ckv.reshape(b, mem_len, -1)
        wr_pad = jnp.zeros((d, LANES), F32).at[:, :n_experts].set(w_router[l]).astype(BF16)
        br_pad = jnp.full((1, LANES), NEG, F32).at[0, :n_experts].set(b_router[l])
        x2, hpack, route, gate, counts = _mid(
            x, att, lru, ckv, norm_att_out_g[l], norm_lru_out_g[l], w_out[l].astype(BF16),
            norm_cross_g[l], w_cq[l].astype(BF16), w_co[l].astype(BF16), norm_ffn_g[l],
            wr_pad, br_pad, n_experts, min(MID_ROWS, seq))
        counts = counts[0, :n_experts].astype(I32)
        padded = (counts + moe_rows - 1) // moe_rows * moe_rows
        pad_end = jnp.cumsum(padded)
        pad_start = pad_end - padded
        expert_ids = jnp.arange(n_experts, dtype=I32)
        sel_e = route[:, :TOP_K]
        start_of = jnp.sum(jnp.where(sel_e[:, :, None] == expert_ids, pad_start, 0), axis=-1)
        dest = start_of + route[:, TOP_K:2 * TOP_K]
        total = pad_end[-1]
        meta = (total // moe_rows).astype(I32).reshape(1)
        tile_start = jnp.minimum(jnp.arange(n_tiles, dtype=I32) * moe_rows, total - moe_rows)
        tile_expert = jnp.sum(tile_start[:, None] >= pad_end[None, :], axis=1).astype(I32)
        later = (expert_ids[None, :] > expert_ids[:, None]) & (padded[None, :] > 0)
        next_expert = jnp.min(jnp.where(later, expert_ids[None, :], n_experts), axis=1)
        next_expert = jnp.where(next_expert == n_experts, -1, next_expert)
        of_tile = lambda table: jnp.sum(
            jnp.where(tile_expert[:, None] == expert_ids, table, 0), axis=1).astype(I32)
        tile_next = of_tile(next_expert)
        in_use = jnp.arange(n_tiles, dtype=I32) * moe_rows < total
        tile_valid = jnp.where(in_use, jnp.clip(
            of_tile(pad_start + counts) - tile_start, 0, moe_rows), 0).astype(I32)
        tiles = (tile_expert, tile_next, tile_valid)
        fill = jnp.concatenate([pad_start + counts, total[None], padded - counts,
                                ((n_slots - total) // (moe_rows // 2))[None]]).astype(I32)
        xs = _dispatch(fill, dest, hpack, n_slots, min(DISPATCH_ROWS, n_tok), moe_rows,
                       d // 2 // LANES)
        cols = min(MOE_COLS, w_down.shape[2] // 2)
        act = _grouped_matmul(tiles, meta, xs, w_gu[l], b_gu[l], moe_rows, cols, True)
        y = _grouped_matmul(tiles, meta, act, w_down[l], b_down[l], moe_rows,
                            min(MOE_COLS, d // 2), False)
        x = _combine(dest, y, x2, gate, norm_final_g, min(COMBINE_ROWS, n_tok)).reshape(b, seq, d)
    return x
```

```python
import functools

import jax
import jax.numpy as jnp
from jax import lax
from jax.experimental import pallas as pl
from jax.experimental.pallas import tpu as pltpu

F32 = jnp.float32
BF16 = jnp.bfloat16
U32 = jnp.uint32
I32 = jnp.int32

EPS = 1e-6
CHUNK = 64
LEFT_CHUNKS = 8
ATT_HEAD_DIM = 64
MAX_REL = 2 * CHUNK
CONV_WIDTH = 4
LRU_C = 8.0
X_HEADS = 4
TOP_K = 4
SWIGLU_ALPHA = 1.702
SWIGLU_LIMIT = 7.0
NEG = -1e30
TINY = 1e-37

LANES = 128
SUBLANES = 8
MXU_COLS = 256
DMA_PRIORITIES = 2
VMEM_LIMIT = 56 * 1024 * 1024

PROJ_ROWS = 512
ATT_QBLOCK = 256
MID_ROWS = 512
MOE_ROWS = 1024
MOE_COLS = 1024
CAST_ROWS = 64
MOE_PART_ROWS = (128, 512)
DISPATCH_ROWS = 512
DISPATCH_GROUP = 8
COMBINE_ROWS = 512
COMBINE_GROUP = 8


def _rms(x, g):
    ms = jnp.mean(x * x, axis=-1, keepdims=True)
    return x * lax.rsqrt(ms + EPS) * g


def _fold_lanes(x, op):
    acc = x[:, :LANES]
    for c in range(LANES, x.shape[-1], LANES):
        acc = op(acc, x[:, c:c + LANES])
    return acc


def _params(*sem):
    return pltpu.CompilerParams(dimension_semantics=sem, vmem_limit_bytes=VMEM_LIMIT)


def _norm_proj_kernel(x_ref, g_ref, w_ref, *out_refs, col_chunk):
    h = _rms(x_ref[...], g_ref[...]).astype(BF16)
    c0 = 0
    for o_ref in out_refs:
        n = o_ref.shape[-1]
        for s in range(0, n, col_chunk):
            e = min(s + col_chunk, n)
            o_ref[:, s:e] = jnp.dot(h, w_ref[:, c0 + s:c0 + e],
                                    preferred_element_type=F32).astype(o_ref.dtype)
        c0 += n


def _norm_proj(x2d, g, w_bf, outs, rows):
    m, d = x2d.shape
    n_total = w_bf.shape[1]
    assert sum(n for n, _ in outs) == n_total and m % rows == 0
    return pl.pallas_call(
        functools.partial(_norm_proj_kernel, col_chunk=512),
        grid=(m // rows,),
        in_specs=[pl.BlockSpec((rows, d), lambda i: (i, 0)),
                  pl.BlockSpec((1, d), lambda i: (0, 0)),
                  pl.BlockSpec((d, n_total), lambda i: (0, 0), pipeline_mode=pl.Buffered(1))],
        out_specs=[pl.BlockSpec((rows, n), lambda i: (i, 0)) for n, _ in outs],
        out_shape=[jax.ShapeDtypeStruct((m, n), dt) for n, dt in outs],
        compiler_params=_params("parallel"),
        name="norm_proj",
    )(x2d, g.reshape(1, d), w_bf)


def _attn_kernel(q_ref, k_ref, v_ref, bias_ref, o_ref, kpad, vpad, *, seq, pad, qb, kb):
    hd = ATT_HEAD_DIM
    kpad[0:pad, :] = jnp.zeros((pad, LANES), BF16)
    vpad[0:pad, :] = jnp.zeros((pad, LANES), BF16)
    kpad[pad:pad + seq, :] = k_ref[0]
    vpad[pad:pad + seq, :] = v_ref[0]
    scale = hd ** -0.5
    kcol = lax.broadcasted_iota(I32, (qb, kb), 1)
    head_of_lane = lax.broadcasted_iota(I32, (qb, LANES), 1) // hd

    def block(ib, carry, *, near_start):
        s0 = pl.multiple_of(ib * qb, qb)
        q = q_ref[0, pl.ds(s0, qb), :] * scale
        kblk = kpad[pl.ds(s0, kb), :]
        vblk = vpad[pl.ds(s0, kb), :]
        out = None
        for hh in range(LANES // hd):
            mine = head_of_lane == hh
            s = lax.dot_general(jnp.where(mine, q, 0.0), kblk, (((1,), (1,)), ((), ())),
                                preferred_element_type=F32)
            s = s + bias_ref[hh]
            if near_start:
                s = jnp.where(kcol < (pad - s0), NEG, s)
            m = jnp.max(_fold_lanes(s, jnp.maximum), axis=-1, keepdims=True)
            p = jnp.exp(s - m)
            l = jnp.sum(_fold_lanes(p, jnp.add), axis=-1, keepdims=True)
            o = jnp.dot(p.astype(BF16), vblk, preferred_element_type=F32) / l
            out = o if out is None else jnp.where(mine, o, out)
        o_ref[0, pl.ds(s0, qb), :] = out.astype(o_ref.dtype)
        return carry

    n_near = min(pad // qb, seq // qb)
    lax.fori_loop(0, n_near, functools.partial(block, near_start=True), 0, unroll=2)
    lax.fori_loop(n_near, seq // qb, functools.partial(block, near_start=False), 0, unroll=True)


def _attention(qkv, rel_table, att_width):
    b, seq, _ = qkv.shape
    heads = att_width // ATT_HEAD_DIM
    hp = LANES // ATT_HEAD_DIM
    qb = min(ATT_QBLOCK, seq)
    pad = LEFT_CHUNKS * CHUNK
    kb = qb + pad
    ql = jnp.arange(qb)[:, None]
    kl = jnp.arange(kb)[None, :]
    span = qb + kb - 1
    rel_u = (qb - 1 + pad) - jnp.arange(span)
    u = rel_table[:, jnp.clip(rel_u, -MAX_REL, MAX_REL) + MAX_REL].astype(F32)
    u = jnp.concatenate([u, jnp.zeros((heads, 1), F32)], axis=1)
    skew = jnp.tile(u, (1, qb))[:, :qb * span].reshape(heads, qb, span)
    bias = skew[:, :, qb - 1:qb - 1 + kb]
    dchunk = ql // CHUNK + LEFT_CHUNKS - kl // CHUNK
    band = (dchunk >= 0) & (dchunk <= LEFT_CHUNKS)
    bias = jnp.where(band[None], bias, NEG)
    nblk = att_width // LANES
    return pl.pallas_call(
        functools.partial(_attn_kernel, seq=seq, pad=pad, qb=qb, kb=kb),
        grid=(b, heads // hp),
        in_specs=[pl.BlockSpec((1, seq, LANES), lambda i, j: (i, 0, j)),
                  pl.BlockSpec((1, seq, LANES), lambda i, j: (i, 0, nblk + j)),
                  pl.BlockSpec((1, seq, LANES), lambda i, j: (i, 0, 2 * nblk + j)),
                  pl.BlockSpec((hp, qb, kb), lambda i, j: (j, 0, 0))],
        out_specs=pl.BlockSpec((1, seq, LANES), lambda i, j: (i, 0, j)),
        out_shape=jax.ShapeDtypeStruct((b, seq, att_width), BF16),
        scratch_shapes=[pltpu.VMEM((seq + pad, LANES), BF16),
                        pltpu.VMEM((seq + pad, LANES), BF16)],
        compiler_params=_params("parallel", "parallel"),
        name="band_attention",
    )(qkv, qkv, qkv, bias)


def _lru_kernel(xr_ref, yg_ref, cw_ref, cb_ref, wg_ref, bg_ref, lam_ref, o_ref,
                a_scr, b_scr, h_scr, *, seq):
    bw = xr_ref.shape[-1]
    x = xr_ref[0]
    row = lax.broadcasted_iota(I32, (seq, bw), 0)
    cw = cw_ref[...]
    xc = cb_ref[...] + cw[CONV_WIDTH - 1:CONV_WIDTH, :] * x
    for j in range(1, CONV_WIDTH):
        xs = jnp.where(row >= j, pltpu.roll(x, j, 0), 0.0)
        xc = xc + cw[CONV_WIDTH - 1 - j:CONV_WIDTH - j, :] * xs
    gates = jnp.dot(xc.astype(BF16), wg_ref[0], preferred_element_type=F32) + bg_ref[0]
    r = 1.0 / (1.0 + jnp.exp(-gates[:, :bw]))
    i = 1.0 / (1.0 + jnp.exp(-gates[:, bw:]))
    z = -lam_ref[...]
    softplus = jnp.maximum(z, 0.0) + jnp.log1p(jnp.exp(-jnp.abs(z)))
    a = jnp.exp(-LRU_C * r * softplus)
    v = 1.0 - a * a
    b = v * lax.rsqrt(jnp.maximum(v, TINY)) * (i * xc)
    a_scr[...] = a
    b_scr[...] = b
    ng = seq // SUBLANES
    ga, gb = [], []
    for j in range(SUBLANES):
        aj = a_scr[pl.ds(j, ng, stride=SUBLANES), :]
        bj = b_scr[pl.ds(j, ng, stride=SUBLANES), :]
        if j:
            bj = aj * gb[-1] + bj
            aj = aj * ga[-1]
        ga.append(aj)
        gb.append(bj)
    ta, tb = ga[-1], gb[-1]
    grow = lax.broadcasted_iota(I32, (ng, bw), 0)
    k = 1
    while k < ng:
        keep = grow >= k
        tb = jnp.where(keep, ta * pltpu.roll(tb, k, 0) + tb, tb)
        ta = jnp.where(keep, ta * pltpu.roll(ta, k, 0), ta)
        k *= 2
    h_in = jnp.where(grow >= 1, pltpu.roll(tb, 1, 0), 0.0)
    for j in range(SUBLANES):
        h_scr[pl.ds(j, ng, stride=SUBLANES), :] = ga[j] * h_in + gb[j]
    g = yg_ref[0]
    gelu = 0.5 * g * (1.0 + jnp.tanh(0.7978845608028654 * (g + 0.044715 * (g * g * g))))
    o_ref[0] = (h_scr[...] * gelu).astype(o_ref.dtype)


def _lru(xy, conv_w, conv_b, w_ga, b_ga, w_gx, b_gx, lam, lru_width):
    b, seq, _ = xy.shape
    nb, bw, _ = w_ga.shape
    wg = jnp.concatenate([w_ga, w_gx], axis=-1).astype(BF16)
    bg = jnp.concatenate([b_ga, b_gx], axis=-1).reshape(nb, 1, 2 * bw)
    return pl.pallas_call(
        functools.partial(_lru_kernel, seq=seq),
        grid=(b, nb),
        in_specs=[pl.BlockSpec((1, seq, bw), lambda i, n: (i, 0, n)),
                  pl.BlockSpec((1, seq, bw), lambda i, n: (i, 0, nb + n)),
                  pl.BlockSpec((CONV_WIDTH, bw), lambda i, n: (0, n)),
                  pl.BlockSpec((1, bw), lambda i, n: (0, n)),
                  pl.BlockSpec((1, bw, 2 * bw), lambda i, n: (n, 0, 0)),
                  pl.BlockSpec((1, 1, 2 * bw), lambda i, n: (n, 0, 0)),
                  pl.BlockSpec((1, bw), lambda i, n: (0, n))],
        out_specs=pl.BlockSpec((1, seq, bw), lambda i, n: (i, 0, n)),
        out_shape=jax.ShapeDtypeStruct((b, seq, lru_width), BF16),
        scratch_shapes=[pltpu.VMEM((seq, bw), F32)] * 3,
        compiler_params=_params("parallel", "parallel"),
        name="rg_lru",
    )(xy, xy, conv_w, conv_b.reshape(1, -1), wg, bg, lam.reshape(1, -1))


def _pack_bf16_pair(lo, hi):
    lo_b = pltpu.bitcast(lo.astype(BF16).astype(F32), U32) >> 16
    hi_b = pltpu.bitcast(hi.astype(BF16).astype(F32), U32) & jnp.uint32(0xFFFF0000)
    return hi_b | lo_b


def _unpack_bf16_pair(w):
    lo = pltpu.bitcast(w << 16, F32)
    hi = pltpu.bitcast(w & jnp.uint32(0xFFFF0000), F32)
    return lo, hi


def _store_token_major(ref, x, nc=None):
    tokens = x.shape[0]
    nc = nc or x.shape[-1] // LANES
    for c in range(x.shape[-1] // LANES):
        ref[pl.ds(c, tokens, stride=nc), :] = x[:, c * LANES:(c + 1) * LANES]


def _load_token_major(load, tokens, nc):
    return jnp.concatenate([load(pl.ds(c, tokens, stride=nc)) for c in range(nc)], axis=-1)


def _mid_kernel(x_ref, att_ref, lru_ref, ck_ref, cv_ref, ga_ref, gl_ref, wo_ref, gc_ref, wcq_ref,
                wco_ref, gf_ref, wr_ref, br_ref,
                x2_ref, hp_ref, route_ref, gate_ref, cnt_ref, cnt_scr, *, n_experts):
    rows, d = x_ref.shape
    aw = att_ref.shape[-1]

    @pl.when((pl.program_id(0) == 0) & (pl.program_id(1) == 0))
    def _():
        cnt_scr[...] = jnp.zeros_like(cnt_scr)

    att_n = _rms(att_ref[...].astype(F32), ga_ref[...]).astype(BF16)
    lru_n = _rms(lru_ref[...].astype(F32), gl_ref[...]).astype(BF16)
    x1 = (x_ref[...]
          + jnp.dot(att_n, wo_ref[0:aw, :], preferred_element_type=F32)
          + jnp.dot(lru_n, wo_ref[aw:, :], preferred_element_type=F32))

    hq = _rms(x1, gc_ref[...]).astype(BF16)
    xw = wcq_ref.shape[-1]
    xhd = xw // X_HEADS
    cq = (jnp.dot(hq, wcq_ref[...], preferred_element_type=F32) * (xhd ** -0.5)).astype(BF16)
    ck = ck_ref[0]
    cv = cv_ref[0]
    heads = []
    for h in range(X_HEADS):
        sl = slice(h * xhd, (h + 1) * xhd)
        s = lax.dot_general(cq[:, sl], ck[:, sl], (((1,), (1,)), ((), ())),
                            preferred_element_type=F32)
        m = jnp.max(s, axis=-1, keepdims=True)
        p = jnp.exp(s - m)
        l = jnp.sum(p, axis=-1, keepdims=True)
        heads.append(jnp.dot(p.astype(BF16), cv[:, sl], preferred_element_type=F32) / l)
    o = jnp.concatenate(heads, axis=-1).astype(BF16)
    x2 = x1 + jnp.dot(o, wco_ref[...], preferred_element_type=F32)
    x2_ref[...] = x2

    hn = _rms(x2, gf_ref[...])
    _store_token_major(hp_ref, _pack_bf16_pair(hn[:, :d // 2], hn[:, d // 2:]))

    logits = jnp.dot(hn.astype(BF16), wr_ref[...], preferred_element_type=F32) + br_ref[...]
    lane = lax.broadcasted_iota(I32, (rows, LANES), 1).astype(F32)
    work = logits
    sel_e, sel_v = [], []
    onehot = jnp.zeros((rows, LANES), F32)
    for _ in range(TOP_K):
        v = jnp.max(work, axis=-1, keepdims=True)
        e = jnp.min(jnp.where(work == v, lane, float(LANES)), axis=-1, keepdims=True)
        hit = lane == e
        onehot = jnp.where(hit, 1.0, onehot)
        work = jnp.where(hit, NEG * 2, work)
        sel_e.append(e)
        sel_v.append(v)
    ex = [jnp.exp(v - sel_v[0]) for v in sel_v]
    den = ex[0] + ex[1] + ex[2] + ex[3]

    ri = lax.broadcasted_iota(I32, (rows, rows), 0)
    ci = lax.broadcasted_iota(I32, (rows, rows), 1)
    tri = jnp.where(ci < ri, 1.0, 0.0).astype(BF16)
    before = jnp.dot(tri, onehot.astype(BF16), preferred_element_type=F32) + cnt_scr[...]
    route = jnp.zeros((rows, LANES), F32)
    gate = jnp.zeros((rows, LANES), F32)
    for k in range(TOP_K):
        rank = jnp.sum(jnp.where(lane == sel_e[k], before, 0.0), axis=-1, keepdims=True)
        route = jnp.where(lane == float(k), sel_e[k], route)
        route = jnp.where(lane == float(TOP_K + k), rank, route)
        gate = jnp.where(lane == float(k), ex[k] / den, gate)
    route_ref[...] = route.astype(I32)
    gate_ref[...] = gate
    cnt_scr[...] = cnt_scr[...] + jnp.sum(onehot, axis=0, keepdims=True)
    cnt_ref[...] = cnt_scr[...]


def _mid(x, att, lru, ckv, ga, gl, wo_bf, gc, wcq_bf, wco_bf, gf, wr_pad, br_pad, n_experts, rows):
    b, seq, d = x.shape
    aw, lw = att.shape[-1], lru.shape[-1]
    mem_len, xw2 = ckv.shape[1], ckv.shape[2]
    xw = xw2 // 2
    n = b * seq
    nt = seq // rows
    row_map = lambda i, t: (i * nt + t, 0)
    const = lambda i, t: (0, 0)
    res = lambda shape: pl.BlockSpec(shape, const, pipeline_mode=pl.Buffered(1))
    return pl.pallas_call(
        functools.partial(_mid_kernel, n_experts=n_experts),
        grid=(b, nt),
        in_specs=[pl.BlockSpec((rows, d), row_map),
                  pl.BlockSpec((rows, aw), row_map),
                  pl.BlockSpec((rows, lw), row_map),
                  pl.BlockSpec((1, mem_len, xw), lambda i, t: (i, 0, 0)),
                  pl.BlockSpec((1, mem_len, xw), lambda i, t: (i, 0, 1)),
                  res((1, aw)), res((1, lw)), res((aw + lw, d)), res((1, d)), res((d, xw)),
                  res((xw, d)), res((1, d)), res((d, LANES)), res((1, LANES))],
        out_specs=[pl.BlockSpec((rows, d), row_map),
                   pl.BlockSpec((rows * (d // 2 // LANES), LANES), row_map),
                   pl.BlockSpec((rows, LANES), row_map),
                   pl.BlockSpec((rows, LANES), row_map),
                   pl.BlockSpec((1, LANES), const)],
        out_shape=[jax.ShapeDtypeStruct((n, d), F32),
                   jax.ShapeDtypeStruct((n * (d // 2 // LANES), LANES), U32),
                   jax.ShapeDtypeStruct((n, LANES), I32),
                   jax.ShapeDtypeStruct((n, LANES), F32),
                   jax.ShapeDtypeStruct((1, LANES), F32)],
        scratch_shapes=[pltpu.VMEM((1, LANES), F32)],
        compiler_params=_params("arbitrary", "arbitrary"),
        name="mix_cross_router",
    )(x.reshape(n, d), att.reshape(n, aw), lru.reshape(n, lw), ckv, ckv,
      ga.reshape(1, aw), gl.reshape(1, lw), wo_bf, gc.reshape(1, d), wcq_bf, wco_bf,
      gf.reshape(1, d), wr_pad, br_pad)


def _dispatch_kernel(fill_ref, dest_ref, src_ref, xs_ref, buf, zbuf, lsem, ssem, zsem,
                     *, rows, nc, n_fill, tail_chunk):
    i = pl.program_id(0)
    n = pl.num_programs(0)
    ztok = zbuf.shape[0] // nc

    def tokens(first, count):
        return pl.ds(pl.multiple_of(first * nc, nc), count * nc)

    def load(t, slot):
        return pltpu.make_async_copy(src_ref.at[tokens(t * rows, rows), :], buf.at[slot],
                                     lsem.at[slot])

    def zero_copy(start, size):
        return pltpu.make_async_copy(zbuf.at[pl.ds(0, size * nc), :],
                                     xs_ref.at[tokens(start, size), :], zsem)

    def fill_chunks(fn):
        for e in range(n_fill):
            start = fill_ref[e]
            length = fill_ref[n_fill + 1 + e]
            size = ztok
            while size >= 1:
                part = length & size
                @pl.when(part != 0)
                def _(start=start, size=size):
                    fn(zero_copy(start, size))
                start = start + part
                size //= 2
        tail_start = fill_ref[n_fill]
        n_tail = fill_ref[2 * n_fill + 1]

        def tail(c, carry):
            fn(zero_copy(tail_start + c * tail_chunk, tail_chunk))
            return carry

        lax.fori_loop(0, n_tail, tail, 0)

    @pl.when(i == 0)
    def _():
        load(0, 0).start()
        zbuf[...] = jnp.zeros_like(zbuf)
        fill_chunks(lambda c: c.start())
        fill_chunks(lambda c: c.wait())

    @pl.when(i + 1 < n)
    def _():
        load(i + 1, (i + 1) % 3).start()

    slot = i % 3
    par = i % 2
    load(i, slot).wait()

    def start_rows(g, c):
        for j in range(DISPATCH_GROUP):
            tok = g * DISPATCH_GROUP + j
            for k in range(TOP_K):
                dst = dest_ref[0, 0, tok * TOP_K + k]
                pltpu.make_async_copy(buf.at[slot, tokens(tok, 1), :],
                                      xs_ref.at[tokens(dst, 1), :],
                                      ssem.at[par]).start(priority=k % DMA_PRIORITIES)
        return c

    lax.fori_loop(0, rows // DISPATCH_GROUP, start_rows, 0)

    def wait_tile(p):
        for _ in range(TOP_K):
            pltpu.make_async_copy(buf.at[0], xs_ref.at[pl.ds(0, rows * nc), :], ssem.at[p]).wait()

    @pl.when(i > 0)
    def _():
        wait_tile(1 - par)

    @pl.when(i == n - 1)
    def _():
        wait_tile(par)


def _dispatch(fill, dest, src, n_slots, rows, moe_rows, nc):
    n_tok = src.shape[0] // nc
    nt = n_tok // rows
    n_fill = (fill.shape[0] - 2) // 2
    ztok = moe_rows // 2
    return pl.pallas_call(
        functools.partial(_dispatch_kernel, rows=rows, nc=nc, n_fill=n_fill, tail_chunk=ztok),
        grid_spec=pltpu.PrefetchScalarGridSpec(
            num_scalar_prefetch=1,
            grid=(nt,),
            in_specs=[pl.BlockSpec((1, 1, rows * TOP_K), lambda i, fill: (i, 0, 0),
                                   memory_space=pltpu.SMEM),
                      pl.BlockSpec(memory_space=pl.ANY)],
            out_specs=pl.BlockSpec(memory_space=pl.ANY),
            scratch_shapes=[pltpu.VMEM((3, rows * nc, LANES), src.dtype),
                            pltpu.VMEM((ztok * nc, LANES), src.dtype),
                            pltpu.SemaphoreType.DMA((3,)),
                            pltpu.SemaphoreType.DMA((2,)),
                            pltpu.SemaphoreType.DMA(())]),
        out_shape=jax.ShapeDtypeStruct((n_slots * nc, LANES), src.dtype),
        compiler_params=_params("arbitrary"),
        name="dispatch_scatter",
    )(fill, dest.reshape(nt, 1, rows * TOP_K), src)


def _grouped_kernel(te_ref, nxt_ref, valid_ref, meta_ref, x_ref, w_ref, ba_ref, bb_ref, o_ref,
                    w_f32, wa_bf, wb_bf, wsem, *, swiglu):
    j = pl.program_id(0)
    i = pl.program_id(1)
    nj = pl.num_programs(0)
    cols = wa_bf.shape[-1]
    expert = te_ref[i]
    new_expert = (i == 0) | (expert != te_ref[jnp.maximum(i - 1, 0)])

    def weight_copy(jj, ee, which):
        col = pl.multiple_of((which * nj + jj) * cols, cols)
        return pltpu.make_async_copy(w_ref.at[ee, :, pl.ds(col, cols)], w_f32.at[which],
                                     wsem.at[which])

    def start_weights(jj, ee):
        weight_copy(jj, ee, 0).start()
        weight_copy(jj, ee, 1).start()

    @pl.when((j == 0) & (i == 0))
    def _():
        start_weights(0, expert)

    @pl.when(new_expert)
    def _():
        weight_copy(j, expert, 0).wait()
        weight_copy(j, expert, 1).wait()
        def cast_rows(c, carry):
            rs = pl.ds(pl.multiple_of(c * CAST_ROWS, CAST_ROWS), CAST_ROWS)
            wa_bf[rs, :] = w_f32[0, rs, :].astype(BF16)
            wb_bf[rs, :] = w_f32[1, rs, :].astype(BF16)
            return carry

        lax.fori_loop(0, wa_bf.shape[0] // CAST_ROWS, cast_rows, 0)
        following = nxt_ref[i]

        @pl.when(following >= 0)
        def _():
            start_weights(j, following)

        @pl.when((following < 0) & (j + 1 < nj))
        def _():
            start_weights(j + 1, te_ref[0])

    tile_rows = o_ref.shape[0] if swiglu else x_ref.shape[0]
    nc_in = x_ref.shape[0] // tile_rows
    nc_out = o_ref.shape[0] // tile_rows

    def compute(rows):
        if swiglu:
            lo, hi = _unpack_bf16_pair(_load_token_major(lambda s: x_ref[s, :], rows, nc_in))
            xb = jnp.concatenate([lo.astype(BF16), hi.astype(BF16)], axis=-1)
        else:
            xb = x_ref[0:rows, :]
        for c0 in range(0, cols, MXU_COLS):
            cs = slice(c0, c0 + MXU_COLS)
            a = jnp.dot(xb, wa_bf[:, cs], preferred_element_type=F32) + ba_ref[0, :, cs]
            b = jnp.dot(xb, wb_bf[:, cs], preferred_element_type=F32) + bb_ref[0, :, cs]
            if swiglu:
                gate = jnp.minimum(a, SWIGLU_LIMIT)
                up = jnp.clip(b, -SWIGLU_LIMIT, SWIGLU_LIMIT)
                act = (up + 1.0) * gate * (1.0 / (1.0 + jnp.exp(-SWIGLU_ALPHA * gate)))
                o_ref[0:rows, cs] = act.astype(o_ref.dtype)
            else:
                packed = _pack_bf16_pair(a, b)
                for c in range(MXU_COLS // LANES):
                    o_ref[pl.ds(c0 // LANES + c, rows, stride=nc_out), :] = (
                        packed[:, c * LANES:(c + 1) * LANES])
        if rows < tile_rows:
            rest = o_ref.shape[0] // tile_rows * rows
            o_ref[rest:, :] = jnp.zeros((o_ref.shape[0] - rest, o_ref.shape[1]), o_ref.dtype)

    valid = valid_ref[i]
    below = 0
    for part in [p for p in MOE_PART_ROWS if p < tile_rows] + [tile_rows]:
        @pl.when((valid > below) & (valid <= part))
        def _(part=part):
            compute(part)
        below = part

    @pl.when(valid == 0)
    def _():
        o_ref[...] = jnp.zeros_like(o_ref)


def _grouped_matmul(tiles, meta, x, w, bias, rows, cols, swiglu):
    n_exp, kdim, n2 = w.shape
    half = n2 // 2
    nj = half // cols
    used = lambda i, meta: jnp.minimum(i, meta[0] - 1)
    if swiglu:
        x_rows = rows * (kdim // 2 // LANES)
        n_slots = x.shape[0] // (kdim // 2 // LANES)
        out_spec = pl.BlockSpec((rows, cols), lambda j, i, te, nx, vl, meta: (i, j))
        out_shape = jax.ShapeDtypeStruct((n_slots, half), BF16)
    else:
        assert nj == 1
        x_rows = rows
        n_slots = x.shape[0]
        out_spec = pl.BlockSpec((rows * (half // LANES), LANES), lambda j, i, te, nx, vl, meta: (i, 0))
        out_shape = jax.ShapeDtypeStruct((n_slots * (half // LANES), LANES), U32)
    n_tiles = n_slots // rows
    return pl.pallas_call(
        functools.partial(_grouped_kernel, swiglu=swiglu),
        grid_spec=pltpu.PrefetchScalarGridSpec(
            num_scalar_prefetch=4,
            grid=(nj, n_tiles),
            in_specs=[pl.BlockSpec((x_rows, x.shape[1]),
                                   lambda j, i, te, nx, vl, meta: (used(i, meta), 0)),
                      pl.BlockSpec(memory_space=pl.ANY),
                      pl.BlockSpec((1, 1, cols), lambda j, i, te, nx, vl, meta: (te[i], 0, j)),
                      pl.BlockSpec((1, 1, cols), lambda j, i, te, nx, vl, meta: (te[i], 0, nj + j))],
            out_specs=out_spec,
            scratch_shapes=[pltpu.VMEM((2, kdim, cols), F32),
                            pltpu.VMEM((kdim, cols), BF16), pltpu.VMEM((kdim, cols), BF16),
                            pltpu.SemaphoreType.DMA((2,))]),
        out_shape=out_shape,
        compiler_params=_params("arbitrary", "arbitrary"),
        name="moe_gate_up" if swiglu else "moe_down",
    )(*tiles, meta, x, w, bias.reshape(n_exp, 1, n2), bias.reshape(n_exp, 1, n2))


def _combine_kernel(dcur_ref, dnext_ref, y_ref, x2_ref, gate_ref, g_ref, o_ref, ybuf, sem, *, rows):
    i = pl.program_id(0)
    n = pl.num_programs(0)

    half = x2_ref.shape[-1] // 2
    nc = half // LANES
    cur = i % 2

    def token(t):
        return pl.ds(pl.multiple_of(t * nc, nc), nc)

    def start_tile(d_ref, slot):
        def step(g, c):
            for j in range(COMBINE_GROUP):
                tok = g * COMBINE_GROUP + j
                for k in range(TOP_K):
                    src = d_ref[0, 0, tok * TOP_K + k]
                    pltpu.make_async_copy(y_ref.at[token(src), :], ybuf.at[slot, k, token(tok), :],
                                          sem.at[slot]).start(priority=k % DMA_PRIORITIES)
            return c

        lax.fori_loop(0, rows // COMBINE_GROUP, step, 0)

    @pl.when(i == 0)
    def _():
        start_tile(dcur_ref, 0)

    @pl.when(i + 1 < n)
    def _():
        start_tile(dnext_ref, 1 - cur)

    for k in range(TOP_K):
        pltpu.make_async_copy(y_ref.at[pl.ds(0, rows * nc), :], ybuf.at[cur, k],
                              sem.at[cur]).wait()

    gate = gate_ref[...]
    acc_lo = x2_ref[:, :half]
    acc_hi = x2_ref[:, half:]
    for k in range(TOP_K):
        lo, hi = _unpack_bf16_pair(
            _load_token_major(lambda s, k=k: ybuf[cur, k, s, :], rows, nc))
        gk = gate[:, k:k + 1]
        acc_lo = acc_lo + gk * lo
        acc_hi = acc_hi + gk * hi
    ms = (jnp.sum(acc_lo * acc_lo, axis=-1, keepdims=True)
          + jnp.sum(acc_hi * acc_hi, axis=-1, keepdims=True)) / (2 * half)
    inv = lax.rsqrt(ms + EPS)
    o_ref[:, :half] = acc_lo * inv * g_ref[:, :half]
    o_ref[:, half:] = acc_hi * inv * g_ref[:, half:]


def _combine(dest, y, x2, gate, g, rows):
    n, d = x2.shape
    nt = n // rows
    return pl.pallas_call(
        functools.partial(_combine_kernel, rows=rows),
        grid=(nt,),
        in_specs=[pl.BlockSpec((1, 1, rows * TOP_K), lambda i: (i, 0, 0), memory_space=pltpu.SMEM),
                  pl.BlockSpec((1, 1, rows * TOP_K), lambda i: (jnp.minimum(i + 1, nt - 1), 0, 0),
                               memory_space=pltpu.SMEM),
                  pl.BlockSpec(memory_space=pl.ANY),
                  pl.BlockSpec((rows, d), lambda i: (i, 0)),
                  pl.BlockSpec((rows, LANES), lambda i: (i, 0)),
                  pl.BlockSpec((1, d), lambda i: (0, 0))],
        out_specs=pl.BlockSpec((rows, d), lambda i: (i, 0)),
        out_shape=jax.ShapeDtypeStruct((n, d), F32),
        scratch_shapes=[pltpu.VMEM((2, TOP_K, rows * (d // 2 // LANES), LANES), U32),
                        pltpu.SemaphoreType.DMA((2,))],
        compiler_params=_params("arbitrary"),
        name="combine_norm",
    )(dest.reshape(nt, 1, rows * TOP_K), dest.reshape(nt, 1, rows * TOP_K), y, x2, gate,
      g.reshape(1, d))


def kernel(x, mem, norm_mix_g, w_in, rel_table, conv_w, conv_b, w_ga, b_ga, w_gx, b_gx, lru_lambda, norm_att_out_g, norm_lru_out_g, w_out, norm_cross_g, norm_mem_g, w_cq, w_ckv, w_co, norm_ffn_g, w_router, b_router, w_gu, b_gu, w_down, b_down, norm_final_g):
    b, seq, d = x.shape
    depth = w_in.shape[0]
    att_width = d // 2
    lru_width = d - att_width
    n_experts = w_router.shape[-1]
    n_tok = b * seq
    mem_len = mem.shape[1]
    assert depth == 1 and n_experts <= LANES and seq % min(ATT_QBLOCK, seq) == 0

    moe_rows = min(MOE_ROWS, n_tok)
    n_tiles = (n_tok * TOP_K + n_experts * (moe_rows - 1)) // moe_rows
    n_slots = n_tiles * moe_rows

    for l in range(depth):
        qkv, xy = _norm_proj(x.reshape(n_tok, d), norm_mix_g[l], w_in[l].astype(BF16),
                             [(3 * att_width, BF16), (2 * lru_width, F32)], min(PROJ_ROWS, seq))
        att = _attention(qkv.reshape(b, seq, 3 * att_width), rel_table[l], att_width)
        lru = _lru(xy.reshape(b, seq, 2 * lru_width), conv_w[l], conv_b[l], w_ga[l], b_ga[l],
                   w_gx[l], b_gx[l], lru_lambda[l], lru_width)
        (ckv,) = _norm_proj(mem.reshape(b * mem_len, d), norm_mem_g[l], w_ckv[l].astype(BF16),
                            [(w_ckv.shape[-1], BF16)], min(PROJ_ROWS, mem_len))
        ckv = ckv.reshape(b, mem_len, -1)
        wr_pad = jnp.zeros((d, LANES), F32).at[:, :n_experts].set(w_router[l]).astype(BF16)
        br_pad = jnp.full((1, LANES), NEG, F32).at[0, :n_experts].set(b_router[l])
        x2, hpack, route, gate, counts = _mid(
            x, att, lru, ckv, norm_att_out_g[l], norm_lru_out_g[l], w_out[l].astype(BF16),
            norm_cross_g[l], w_cq[l].astype(BF16), w_co[l].astype(BF16), norm_ffn_g[l],
            wr_pad, br_pad, n_experts, min(MID_ROWS, seq))
        counts = counts[0, :n_experts].astype(I32)
        padded = (counts + moe_rows - 1) // moe_rows * moe_rows
        pad_end = jnp.cumsum(padded)
        pad_start = pad_end - padded
        expert_ids = jnp.arange(n_experts, dtype=I32)
        sel_e = route[:, :TOP_K]
        start_of = jnp.sum(jnp.where(sel_e[:, :, None] == expert_ids, pad_start, 0), axis=-1)
        dest = start_of + route[:, TOP_K:2 * TOP_K]
        total = pad_end[-1]
        meta = (total // moe_rows).astype(I32).reshape(1)
        tile_start = jnp.minimum(jnp.arange(n_tiles, dtype=I32) * moe_rows, total - moe_rows)
        tile_expert = jnp.sum(tile_start[:, None] >= pad_end[None, :], axis=1).astype(I32)
        later = (expert_ids[None, :] > expert_ids[:, None]) & (padded[None, :] > 0)
        next_expert = jnp.min(jnp.where(later, expert_ids[None, :], n_experts), axis=1)
        next_expert = jnp.where(next_expert == n_experts, -1, next_expert)
        of_tile = lambda table: jnp.sum(
            jnp.where(tile_expert[:, None] == expert_ids, table, 0), axis=1).astype(I32)
        tile_next = of_tile(next_expert)
        in_use = jnp.arange(n_tiles, dtype=I32) * moe_rows < total
        tile_valid = jnp.where(in_use, jnp.clip(
            of_tile(pad_start + counts) - tile_start, 0, moe_rows), 0).astype(I32)
        tiles = (tile_expert, tile_next, tile_valid)
        fill = jnp.concatenate([pad_start + counts, total[None], padded - counts,
                                ((n_slots - total) // (moe_rows // 2))[None]]).astype(I32)
        xs = _dispatch(fill, dest, hpack, n_slots, min(DISPATCH_ROWS, n_tok), moe_rows,
                       d // 2 // LANES)
        cols = min(MOE_COLS, w_down.shape[2] // 2)
        act = _grouped_matmul(tiles, meta, xs, w_gu[l], b_gu[l], moe_rows, cols, True)
        y = _grouped_matmul(tiles, meta, act, w_down[l], b_down[l], moe_rows,
                            min(MOE_COLS, d // 2), False)
        x = _combine(dest, y, x2, gate, norm_final_g, min(COMBINE_ROWS, n_tok)).reshape(b, seq, d)
    return x
```

```python
import functools

import jax
import jax.numpy as jnp
from jax import lax
from jax.experimental import pallas as pl
from jax.experimental.pallas import tpu as pltpu

F32 = jnp.float32
BF16 = jnp.bfloat16
U32 = jnp.uint32
I32 = jnp.int32

EPS = 1e-6
CHUNK = 64
LEFT_CHUNKS = 8
ATT_HEAD_DIM = 64
MAX_REL = 2 * CHUNK
CONV_WIDTH = 4
LRU_C = 8.0
X_HEADS = 4
TOP_K = 4
SWIGLU_ALPHA = 1.702
SWIGLU_LIMIT = 7.0
NEG = -1e30
TINY = 1e-37

LANES = 128
SUBLANES = 8
MXU_COLS = 256
DMA_PRIORITIES = 2
VMEM_LIMIT = 56 * 1024 * 1024

PROJ_ROWS = 512
ATT_QBLOCK = 256
MID_ROWS = 512
MOE_ROWS = 1024
MOE_COLS = 1024
CAST_ROWS = 64
MOE_PART_ROWS = (128, 512)
DISPATCH_ROWS = 256
DISPATCH_GROUP = 8
COMBINE_ROWS = 256
COMBINE_GROUP = 8


def _rms(x, g):
    ms = jnp.mean(x * x, axis=-1, keepdims=True)
    return x * lax.rsqrt(ms + EPS) * g


def _fold_lanes(x, op):
    acc = x[:, :LANES]
    for c in range(LANES, x.shape[-1], LANES):
        acc = op(acc, x[:, c:c + LANES])
    return acc


def _params(*sem):
    return pltpu.CompilerParams(dimension_semantics=sem, vmem_limit_bytes=VMEM_LIMIT)


def _norm_proj_kernel(x_ref, g_ref, w_ref, *out_refs, col_chunk):
    h = _rms(x_ref[...], g_ref[...]).astype(BF16)
    c0 = 0
    for o_ref in out_refs:
        n = o_ref.shape[-1]
        for s in range(0, n, col_chunk):
            e = min(s + col_chunk, n)
            o_ref[:, s:e] = jnp.dot(h, w_ref[:, c0 + s:c0 + e],
                                    preferred_element_type=F32).astype(o_ref.dtype)
        c0 += n


def _norm_proj(x2d, g, w_bf, outs, rows):
    m, d = x2d.shape
    n_total = w_bf.shape[1]
    assert sum(n for n, _ in outs) == n_total and m % rows == 0
    return pl.pallas_call(
        functools.partial(_norm_proj_kernel, col_chunk=512),
        grid=(m // rows,),
        in_specs=[pl.BlockSpec((rows, d), lambda i: (i, 0)),
                  pl.BlockSpec((1, d), lambda i: (0, 0)),
                  pl.BlockSpec((d, n_total), lambda i: (0, 0), pipeline_mode=pl.Buffered(1))],
        out_specs=[pl.BlockSpec((rows, n), lambda i: (i, 0)) for n, _ in outs],
        out_shape=[jax.ShapeDtypeStruct((m, n), dt) for n, dt in outs],
        compiler_params=_params("parallel"),
        name="norm_proj",
    )(x2d, g.reshape(1, d), w_bf)


def _attn_kernel(q_ref, k_ref, v_ref, bias_ref, o_ref, kpad, vpad, *, seq, pad, qb, kb):
    hd = ATT_HEAD_DIM
    kpad[0:pad, :] = jnp.zeros((pad, LANES), BF16)
    vpad[0:pad, :] = jnp.zeros((pad, LANES), BF16)
    kpad[pad:pad + seq, :] = k_ref[0]
    vpad[pad:pad + seq, :] = v_ref[0]
    scale = hd ** -0.5
    kcol = lax.broadcasted_iota(I32, (qb, kb), 1)
    head_of_lane = lax.broadcasted_iota(I32, (qb, LANES), 1) // hd

    def block(ib, carry, *, near_start):
        s0 = pl.multiple_of(ib * qb, qb)
        q = q_ref[0, pl.ds(s0, qb), :] * scale
        kblk = kpad[pl.ds(s0, kb), :]
        vblk = vpad[pl.ds(s0, kb), :]
        out = None
        for hh in range(LANES // hd):
            mine = head_of_lane == hh
            s = lax.dot_general(jnp.where(mine, q, 0.0), kblk, (((1,), (1,)), ((), ())),
                                preferred_element_type=F32)
            s = s + bias_ref[hh]
            if near_start:
                s = jnp.where(kcol < (pad - s0), NEG, s)
            m = jnp.max(_fold_lanes(s, jnp.maximum), axis=-1, keepdims=True)
            p = jnp.exp(s - m)
            l = jnp.sum(_fold_lanes(p, jnp.add), axis=-1, keepdims=True)
            o = jnp.dot(p.astype(BF16), vblk, preferred_element_type=F32) / l
            out = o if out is None else jnp.where(mine, o, out)
        o_ref[0, pl.ds(s0, qb), :] = out.astype(o_ref.dtype)
        return carry

    n_near = min(pad // qb, seq // qb)
    lax.fori_loop(0, n_near, functools.partial(block, near_start=True), 0, unroll=2)
    lax.fori_loop(n_near, seq // qb, functools.partial(block, near_start=False), 0, unroll=True)


def _attention(qkv, rel_table, att_width):
    b, seq, _ = qkv.shape
    heads = att_width // ATT_HEAD_DIM
    hp = LANES // ATT_HEAD_DIM
    qb = min(ATT_QBLOCK, seq)
    pad = LEFT_CHUNKS * CHUNK
    kb = qb + pad
    ql = jnp.arange(qb)[:, None]
    kl = jnp.arange(kb)[None, :]
    span = qb + kb - 1
    rel_u = (qb - 1 + pad) - jnp.arange(span)
    u = rel_table[:, jnp.clip(rel_u, -MAX_REL, MAX_REL) + MAX_REL].astype(F32)
    u = jnp.concatenate([u, jnp.zeros((heads, 1), F32)], axis=1)
    skew = jnp.tile(u, (1, qb))[:, :qb * span].reshape(heads, qb, span)
    bias = skew[:, :, qb - 1:qb - 1 + kb]
    dchunk = ql // CHUNK + LEFT_CHUNKS - kl // CHUNK
    band = (dchunk >= 0) & (dchunk <= LEFT_CHUNKS)
    bias = jnp.where(band[None], bias, NEG)
    nblk = att_width // LANES
    return pl.pallas_call(
        functools.partial(_attn_kernel, seq=seq, pad=pad, qb=qb, kb=kb),
        grid=(b, heads // hp),
        in_specs=[pl.BlockSpec((1, seq, LANES), lambda i, j: (i, 0, j)),
                  pl.BlockSpec((1, seq, LANES), lambda i, j: (i, 0, nblk + j)),
                  pl.BlockSpec((1, seq, LANES), lambda i, j: (i, 0, 2 * nblk + j)),
                  pl.BlockSpec((hp, qb, kb), lambda i, j: (j, 0, 0))],
        out_specs=pl.BlockSpec((1, seq, LANES), lambda i, j: (i, 0, j)),
        out_shape=jax.ShapeDtypeStruct((b, seq, att_width), BF16),
        scratch_shapes=[pltpu.VMEM((seq + pad, LANES), BF16),
                        pltpu.VMEM((seq + pad, LANES), BF16)],
        compiler_params=_params("parallel", "parallel"),
        name="band_attention",
    )(qkv, qkv, qkv, bias)


def _lru_kernel(xr_ref, yg_ref, cw_ref, cb_ref, wg_ref, bg_ref, lam_ref, o_ref,
                a_scr, b_scr, h_scr, *, seq):
    bw = xr_ref.shape[-1]
    x = xr_ref[0]
    row = lax.broadcasted_iota(I32, (seq, bw), 0)
    cw = cw_ref[...]
    xc = cb_ref[...] + cw[CONV_WIDTH - 1:CONV_WIDTH, :] * x
    for j in range(1, CONV_WIDTH):
        xs = jnp.where(row >= j, pltpu.roll(x, j, 0), 0.0)
        xc = xc + cw[CONV_WIDTH - 1 - j:CONV_WIDTH - j, :] * xs
    gates = jnp.dot(xc.astype(BF16), wg_ref[0], preferred_element_type=F32) + bg_ref[0]
    r = 1.0 / (1.0 + jnp.exp(-gates[:, :bw]))
    i = 1.0 / (1.0 + jnp.exp(-gates[:, bw:]))
    z = -lam_ref[...]
    softplus = jnp.maximum(z, 0.0) + jnp.log1p(jnp.exp(-jnp.abs(z)))
    a = jnp.exp(-LRU_C * r * softplus)
    v = 1.0 - a * a
    b = v * lax.rsqrt(jnp.maximum(v, TINY)) * (i * xc)
    a_scr[...] = a
    b_scr[...] = b
    ng = seq // SUBLANES
    ga, gb = [], []
    for j in range(SUBLANES):
        aj = a_scr[pl.ds(j, ng, stride=SUBLANES), :]
        bj = b_scr[pl.ds(j, ng, stride=SUBLANES), :]
        if j:
            bj = aj * gb[-1] + bj
            aj = aj * ga[-1]
        ga.append(aj)
        gb.append(bj)
    ta, tb = ga[-1], gb[-1]
    grow = lax.broadcasted_iota(I32, (ng, bw), 0)
    k = 1
    while k < ng:
        keep = grow >= k
        tb = jnp.where(keep, ta * pltpu.roll(tb, k, 0) + tb, tb)
        ta = jnp.where(keep, ta * pltpu.roll(ta, k, 0), ta)
        k *= 2
    h_in = jnp.where(grow >= 1, pltpu.roll(tb, 1, 0), 0.0)
    for j in range(SUBLANES):
        h_scr[pl.ds(j, ng, stride=SUBLANES), :] = ga[j] * h_in + gb[j]
    g = yg_ref[0]
    gelu = 0.5 * g * (1.0 + jnp.tanh(0.7978845608028654 * (g + 0.044715 * (g * g * g))))
    o_ref[0] = (h_scr[...] * gelu).astype(o_ref.dtype)


def _lru(xy, conv_w, conv_b, w_ga, b_ga, w_gx, b_gx, lam, lru_width):
    b, seq, _ = xy.shape
    nb, bw, _ = w_ga.shape
    wg = jnp.concatenate([w_ga, w_gx], axis=-1).astype(BF16)
    bg = jnp.concatenate([b_ga, b_gx], axis=-1).reshape(nb, 1, 2 * bw)
    return pl.pallas_call(
        functools.partial(_lru_kernel, seq=seq),
        grid=(b, nb),
        in_specs=[pl.BlockSpec((1, seq, bw), lambda i, n: (i, 0, n)),
                  pl.BlockSpec((1, seq, bw), lambda i, n: (i, 0, nb + n)),
                  pl.BlockSpec((CONV_WIDTH, bw), lambda i, n: (0, n)),
                  pl.BlockSpec((1, bw), lambda i, n: (0, n)),
                  pl.BlockSpec((1, bw, 2 * bw), lambda i, n: (n, 0, 0)),
                  pl.BlockSpec((1, 1, 2 * bw), lambda i, n: (n, 0, 0)),
                  pl.BlockSpec((1, bw), lambda i, n: (0, n))],
        out_specs=pl.BlockSpec((1, seq, bw), lambda i, n: (i, 0, n)),
        out_shape=jax.ShapeDtypeStruct((b, seq, lru_width), BF16),
        scratch_shapes=[pltpu.VMEM((seq, bw), F32)] * 3,
        compiler_params=_params("parallel", "parallel"),
        name="rg_lru",
    )(xy, xy, conv_w, conv_b.reshape(1, -1), wg, bg, lam.reshape(1, -1))


def _pack_bf16_pair(lo, hi):
    lo_b = pltpu.bitcast(lo.astype(BF16).astype(F32), U32) >> 16
    hi_b = pltpu.bitcast(hi.astype(BF16).astype(F32), U32) & jnp.uint32(0xFFFF0000)
    return hi_b | lo_b


def _unpack_bf16_pair(w):
    lo = pltpu.bitcast(w << 16, F32)
    hi = pltpu.bitcast(w & jnp.uint32(0xFFFF0000), F32)
    return lo, hi


def _store_token_major(ref, x, nc=None):
    tokens = x.shape[0]
    nc = nc or x.shape[-1] // LANES
    for c in range(x.shape[-1] // LANES):
        ref[pl.ds(c, tokens, stride=nc), :] = x[:, c * LANES:(c + 1) * LANES]


def _load_token_major(load, tokens, nc):
    return jnp.concatenate([load(pl.ds(c, tokens, stride=nc)) for c in range(nc)], axis=-1)


def _mid_kernel(x_ref, att_ref, lru_ref, ck_ref, cv_ref, ga_ref, gl_ref, wo_ref, gc_ref, wcq_ref,
                wco_ref, gf_ref, wr_ref, br_ref,
                x2_ref, hp_ref, route_ref, gate_ref, cnt_ref, cnt_scr, *, n_experts):
    rows, d = x_ref.shape
    aw = att_ref.shape[-1]

    @pl.when((pl.program_id(0) == 0) & (pl.program_id(1) == 0))
    def _():
        cnt_scr[...] = jnp.zeros_like(cnt_scr)

    att_n = _rms(att_ref[...].astype(F32), ga_ref[...]).astype(BF16)
    lru_n = _rms(lru_ref[...].astype(F32), gl_ref[...]).astype(BF16)
    x1 = (x_ref[...]
          + jnp.dot(att_n, wo_ref[0:aw, :], preferred_element_type=F32)
          + jnp.dot(lru_n, wo_ref[aw:, :], preferred_element_type=F32))

    hq = _rms(x1, gc_ref[...]).astype(BF16)
    xw = wcq_ref.shape[-1]
    xhd = xw // X_HEADS
    cq = (jnp.dot(hq, wcq_ref[...], preferred_element_type=F32) * (xhd ** -0.5)).astype(BF16)
    ck = ck_ref[0]
    cv = cv_ref[0]
    heads = []
    for h in range(X_HEADS):
        sl = slice(h * xhd, (h + 1) * xhd)
        s = lax.dot_general(cq[:, sl], ck[:, sl], (((1,), (1,)), ((), ())),
                            preferred_element_type=F32)
        m = jnp.max(s, axis=-1, keepdims=True)
        p = jnp.exp(s - m)
        l = jnp.sum(p, axis=-1, keepdims=True)
        heads.append(jnp.dot(p.astype(BF16), cv[:, sl], preferred_element_type=F32) / l)
    o = jnp.concatenate(heads, axis=-1).astype(BF16)
    x2 = x1 + jnp.dot(o, wco_ref[...], preferred_element_type=F32)
    x2_ref[...] = x2

    hn = _rms(x2, gf_ref[...])
    _store_token_major(hp_ref, _pack_bf16_pair(hn[:, :d // 2], hn[:, d // 2:]))

    logits = jnp.dot(hn.astype(BF16), wr_ref[...], preferred_element_type=F32) + br_ref[...]
    lane = lax.broadcasted_iota(I32, (rows, LANES), 1).astype(F32)
    work = logits
    sel_e, sel_v = [], []
    onehot = jnp.zeros((rows, LANES), F32)
    for _ in range(TOP_K):
        v = jnp.max(work, axis=-1, keepdims=True)
        e = jnp.min(jnp.where(work == v, lane, float(LANES)), axis=-1, keepdims=True)
        hit = lane == e
        onehot = jnp.where(hit, 1.0, onehot)
        work = jnp.where(hit, NEG * 2, work)
        sel_e.append(e)
        sel_v.append(v)
    ex = [jnp.exp(v - sel_v[0]) for v in sel_v]
    den = ex[0] + ex[1] + ex[2] + ex[3]

    ri = lax.broadcasted_iota(I32, (rows, rows), 0)
    ci = lax.broadcasted_iota(I32, (rows, rows), 1)
    tri = jnp.where(ci < ri, 1.0, 0.0).astype(BF16)
    before = jnp.dot(tri, onehot.astype(BF16), preferred_element_type=F32) + cnt_scr[...]
    route = jnp.zeros((rows, LANES), F32)
    gate = jnp.zeros((rows, LANES), F32)
    for k in range(TOP_K):
        rank = jnp.sum(jnp.where(lane == sel_e[k], before, 0.0), axis=-1, keepdims=True)
        route = jnp.where(lane == float(k), sel_e[k], route)
        route = jnp.where(lane == float(TOP_K + k), rank, route)
        gate = jnp.where(lane == float(k), ex[k] / den, gate)
    route_ref[...] = route.astype(I32)
    gate_ref[...] = gate
    cnt_scr[...] = cnt_scr[...] + jnp.sum(onehot, axis=0, keepdims=True)
    cnt_ref[...] = cnt_scr[...]


def _mid(x, att, lru, ckv, ga, gl, wo_bf, gc, wcq_bf, wco_bf, gf, wr_pad, br_pad, n_experts, rows):
    b, seq, d = x.shape
    aw, lw = att.shape[-1], lru.shape[-1]
    mem_len, xw2 = ckv.shape[1], ckv.shape[2]
    xw = xw2 // 2
    n = b * seq
    nt = seq // rows
    row_map = lambda i, t: (i * nt + t, 0)
    const = lambda i, t: (0, 0)
    res = lambda shape: pl.BlockSpec(shape, const, pipeline_mode=pl.Buffered(1))
    return pl.pallas_call(
        functools.partial(_mid_kernel, n_experts=n_experts),
        grid=(b, nt),
        in_specs=[pl.BlockSpec((rows, d), row_map),
                  pl.BlockSpec((rows, aw), row_map),
                  pl.BlockSpec((rows, lw), row_map),
                  pl.BlockSpec((1, mem_len, xw), lambda i, t: (i, 0, 0)),
                  pl.BlockSpec((1, mem_len, xw), lambda i, t: (i, 0, 1)),
                  res((1, aw)), res((1, lw)), res((aw + lw, d)), res((1, d)), res((d, xw)),
                  res((xw, d)), res((1, d)), res((d, LANES)), res((1, LANES))],
        out_specs=[pl.BlockSpec((rows, d), row_map),
                   pl.BlockSpec((rows * (d // 2 // LANES), LANES), row_map),
                   pl.BlockSpec((rows, LANES), row_map),
                   pl.BlockSpec((rows, LANES), row_map),
                   pl.BlockSpec((1, LANES), const)],
        out_shape=[jax.ShapeDtypeStruct((n, d), F32),
                   jax.ShapeDtypeStruct((n * (d // 2 // LANES), LANES), U32),
                   jax.ShapeDtypeStruct((n, LANES), I32),
                   jax.ShapeDtypeStruct((n, LANES), F32),
                   jax.ShapeDtypeStruct((1, LANES), F32)],
        scratch_shapes=[pltpu.VMEM((1, LANES), F32)],
        compiler_params=_params("arbitrary", "arbitrary"),
        name="mix_cross_router",
    )(x.reshape(n, d), att.reshape(n, aw), lru.reshape(n, lw), ckv, ckv,
      ga.reshape(1, aw), gl.reshape(1, lw), wo_bf, gc.reshape(1, d), wcq_bf, wco_bf,
      gf.reshape(1, d), wr_pad, br_pad)


def _dispatch_kernel(fill_ref, dest_ref, src_ref, xs_ref, buf, zbuf, lsem, ssem, zsem,
                     *, rows, nc, n_fill, tail_chunk):
    i = pl.program_id(0)
    n = pl.num_programs(0)
    ztok = zbuf.shape[0] // nc

    def tokens(first, count):
        return pl.ds(pl.multiple_of(first * nc, nc), count * nc)

    def load(t, slot):
        return pltpu.make_async_copy(src_ref.at[tokens(t * rows, rows), :], buf.at[slot],
                                     lsem.at[slot])

    def zero_copy(start, size):
        return pltpu.make_async_copy(zbuf.at[pl.ds(0, size * nc), :],
                                     xs_ref.at[tokens(start, size), :], zsem)

    def fill_chunks(fn):
        for e in range(n_fill):
            start = fill_ref[e]
            length = fill_ref[n_fill + 1 + e]
            size = ztok
            while size >= 1:
                part = length & size
                @pl.when(part != 0)
                def _(start=start, size=size):
                    fn(zero_copy(start, size))
                start = start + part
                size //= 2
        tail_start = fill_ref[n_fill]
        n_tail = fill_ref[2 * n_fill + 1]

        def tail(c, carry):
            fn(zero_copy(tail_start + c * tail_chunk, tail_chunk))
            return carry

        lax.fori_loop(0, n_tail, tail, 0)

    @pl.when(i == 0)
    def _():
        load(0, 0).start()
        zbuf[...] = jnp.zeros_like(zbuf)
        fill_chunks(lambda c: c.start())
        fill_chunks(lambda c: c.wait())

    @pl.when(i + 1 < n)
    def _():
        load(i + 1, (i + 1) % 3).start()

    slot = i % 3
    par = i % 2
    load(i, slot).wait()

    def start_rows(g, c):
        for j in range(DISPATCH_GROUP):
            tok = g * DISPATCH_GROUP + j
            for k in range(TOP_K):
                dst = dest_ref[0, 0, tok * TOP_K + k]
                pltpu.make_async_copy(buf.at[slot, tokens(tok, 1), :],
                                      xs_ref.at[tokens(dst, 1), :],
                                      ssem.at[par]).start(priority=k % DMA_PRIORITIES)
        return c

    lax.fori_loop(0, rows // DISPATCH_GROUP, start_rows, 0)

    def wait_tile(p):
        for _ in range(TOP_K):
            pltpu.make_async_copy(buf.at[0], xs_ref.at[pl.ds(0, rows * nc), :], ssem.at[p]).wait()

    @pl.when(i > 0)
    def _():
        wait_tile(1 - par)

    @pl.when(i == n - 1)
    def _():
        wait_tile(par)


def _dispatch(fill, dest, src, n_slots, rows, moe_rows, nc):
    n_tok = src.shape[0] // nc
    nt = n_tok // rows
    n_fill = (fill.shape[0] - 2) // 2
    ztok = moe_rows // 2
    return pl.pallas_call(
        functools.partial(_dispatch_kernel, rows=rows, nc=nc, n_fill=n_fill, tail_chunk=ztok),
        grid_spec=pltpu.PrefetchScalarGridSpec(
            num_scalar_prefetch=1,
            grid=(nt,),
            in_specs=[pl.BlockSpec((1, 1, rows * TOP_K), lambda i, fill: (i, 0, 0),
                                   memory_space=pltpu.SMEM),
                      pl.BlockSpec(memory_space=pl.ANY)],
            out_specs=pl.BlockSpec(memory_space=pl.ANY),
            scratch_shapes=[pltpu.VMEM((3, rows * nc, LANES), src.dtype),
                            pltpu.VMEM((ztok * nc, LANES), src.dtype),
                            pltpu.SemaphoreType.DMA((3,)),
                            pltpu.SemaphoreType.DMA((2,)),
                            pltpu.SemaphoreType.DMA(())]),
        out_shape=jax.ShapeDtypeStruct((n_slots * nc, LANES), src.dtype),
        compiler_params=_params("arbitrary"),
        name="dispatch_scatter",
    )(fill, dest.reshape(nt, 1, rows * TOP_K), src)


def _grouped_kernel(te_ref, nxt_ref, valid_ref, meta_ref, x_ref, w_ref, ba_ref, bb_ref, o_ref,
                    w_f32, wa_bf, wb_bf, wsem, *, swiglu):
    j = pl.program_id(0)
    i = pl.program_id(1)
    nj = pl.num_programs(0)
    cols = wa_bf.shape[-1]
    expert = te_ref[i]
    new_expert = (i == 0) | (expert != te_ref[jnp.maximum(i - 1, 0)])

    def weight_copy(jj, ee, which):
        col = pl.multiple_of((which * nj + jj) * cols, cols)
        return pltpu.make_async_copy(w_ref.at[ee, :, pl.ds(col, cols)], w_f32.at[which],
                                     wsem.at[which])

    def start_weights(jj, ee):
        weight_copy(jj, ee, 0).start()
        weight_copy(jj, ee, 1).start()

    @pl.when((j == 0) & (i == 0))
    def _():
        start_weights(0, expert)

    @pl.when(new_expert)
    def _():
        weight_copy(j, expert, 0).wait()
        weight_copy(j, expert, 1).wait()
        def cast_rows(c, carry):
            rs = pl.ds(pl.multiple_of(c * CAST_ROWS, CAST_ROWS), CAST_ROWS)
            wa_bf[rs, :] = w_f32[0, rs, :].astype(BF16)
            wb_bf[rs, :] = w_f32[1, rs, :].astype(BF16)
            return carry

        lax.fori_loop(0, wa_bf.shape[0] // CAST_ROWS, cast_rows, 0)
        following = nxt_ref[i]

        @pl.when(following >= 0)
        def _():
            start_weights(j, following)

        @pl.when((following < 0) & (j + 1 < nj))
        def _():
            start_weights(j + 1, te_ref[0])

    tile_rows = o_ref.shape[0] if swiglu else x_ref.shape[0]
    nc_in = x_ref.shape[0] // tile_rows
    nc_out = o_ref.shape[0] // tile_rows

    def compute(rows):
        if swiglu:
            lo, hi = _unpack_bf16_pair(_load_token_major(lambda s: x_ref[s, :], rows, nc_in))
            xb = jnp.concatenate([lo.astype(BF16), hi.astype(BF16)], axis=-1)
        else:
            xb = x_ref[0:rows, :]
        for c0 in range(0, cols, MXU_COLS):
            cs = slice(c0, c0 + MXU_COLS)
            a = jnp.dot(xb, wa_bf[:, cs], preferred_element_type=F32) + ba_ref[0, :, cs]
            b = jnp.dot(xb, wb_bf[:, cs], preferred_element_type=F32) + bb_ref[0, :, cs]
            if swiglu:
                gate = jnp.minimum(a, SWIGLU_LIMIT)
                up = jnp.clip(b, -SWIGLU_LIMIT, SWIGLU_LIMIT)
                act = (up + 1.0) * gate * (1.0 / (1.0 + jnp.exp(-SWIGLU_ALPHA * gate)))
                o_ref[0:rows, cs] = act.astype(o_ref.dtype)
            else:
                packed = _pack_bf16_pair(a, b)
                for c in range(MXU_COLS // LANES):
                    o_ref[pl.ds(c0 // LANES + c, rows, stride=nc_out), :] = (
                        packed[:, c * LANES:(c + 1) * LANES])
        if rows < tile_rows:
            rest = o_ref.shape[0] // tile_rows * rows
            o_ref[rest:, :] = jnp.zeros((o_ref.shape[0] - rest, o_ref.shape[1]), o_ref.dtype)

    valid = valid_ref[i]
    below = 0
    for part in [p for p in MOE_PART_ROWS if p < tile_rows] + [tile_rows]:
        @pl.when((valid > below) & (valid <= part))
        def _(part=part):
            compute(part)
        below = part

    @pl.when(valid == 0)
    def _():
        o_ref[...] = jnp.zeros_like(o_ref)


def _grouped_matmul(tiles, meta, x, w, bias, rows, cols, swiglu):
    n_exp, kdim, n2 = w.shape
    half = n2 // 2
    nj = half // cols
    used = lambda i, meta: jnp.minimum(i, meta[0] - 1)
    if swiglu:
        x_rows = rows * (kdim // 2 // LANES)
        n_slots = x.shape[0] // (kdim // 2 // LANES)
        out_spec = pl.BlockSpec((rows, cols), lambda j, i, te, nx, vl, meta: (i, j))
        out_shape = jax.ShapeDtypeStruct((n_slots, half), BF16)
    else:
        assert nj == 1
        x_rows = rows
        n_slots = x.shape[0]
        out_spec = pl.BlockSpec((rows * (half // LANES), LANES), lambda j, i, te, nx, vl, meta: (i, 0))
        out_shape = jax.ShapeDtypeStruct((n_slots * (half // LANES), LANES), U32)
    n_tiles = n_slots // rows
    return pl.pallas_call(
        functools.partial(_grouped_kernel, swiglu=swiglu),
        grid_spec=pltpu.PrefetchScalarGridSpec(
            num_scalar_prefetch=4,
            grid=(nj, n_tiles),
            in_specs=[pl.BlockSpec((x_rows, x.shape[1]),
                                   lambda j, i, te, nx, vl, meta: (used(i, meta), 0)),
                      pl.BlockSpec(memory_space=pl.ANY),
                      pl.BlockSpec((1, 1, cols), lambda j, i, te, nx, vl, meta: (te[i], 0, j)),
                      pl.BlockSpec((1, 1, cols), lambda j, i, te, nx, vl, meta: (te[i], 0, nj + j))],
            out_specs=out_spec,
            scratch_shapes=[pltpu.VMEM((2, kdim, cols), F32),
                            pltpu.VMEM((kdim, cols), BF16), pltpu.VMEM((kdim, cols), BF16),
                            pltpu.SemaphoreType.DMA((2,))]),
        out_shape=out_shape,
        compiler_params=_params("arbitrary", "arbitrary"),
        name="moe_gate_up" if swiglu else "moe_down",
    )(*tiles, meta, x, w, bias.reshape(n_exp, 1, n2), bias.reshape(n_exp, 1, n2))


def _combine_kernel(dcur_ref, dnext_ref, y_ref, x2_ref, gate_ref, g_ref, o_ref, ybuf, sem, *, rows):
    i = pl.program_id(0)
    n = pl.num_programs(0)

    half = x2_ref.shape[-1] // 2
    nc = half // LANES
    cur = i % 2

    def token(t):
        return pl.ds(pl.multiple_of(t * nc, nc), nc)

    n_groups = rows // COMBINE_GROUP

    def start_tile(d_ref, slot, first_group, last_group):
        def step(g, c):
            for j in range(COMBINE_GROUP):
                tok = g * COMBINE_GROUP + j
                for k in range(TOP_K):
                    src = d_ref[0, 0, tok * TOP_K + k]
                    pltpu.make_async_copy(y_ref.at[token(src), :], ybuf.at[slot, k, token(tok), :],
                                          sem.at[slot]).start(priority=k % DMA_PRIORITIES)
            return c

        lax.fori_loop(first_group, last_group, step, 0)

    @pl.when(i == 0)
    def _():
        start_tile(dcur_ref, 0, 0, n_groups)

    @pl.when(i + 1 < n)
    def _():
        start_tile(dnext_ref, 1 - cur, 0, n_groups // 2)

    for k in range(TOP_K):
        pltpu.make_async_copy(y_ref.at[pl.ds(0, rows * nc), :], ybuf.at[cur, k],
                              sem.at[cur]).wait()

    gate = gate_ref[...]
    acc_lo = x2_ref[:, :half]
    acc_hi = x2_ref[:, half:]
    for k in range(TOP_K):
        lo, hi = _unpack_bf16_pair(
            _load_token_major(lambda s, k=k: ybuf[cur, k, s, :], rows, nc))
        gk = gate[:, k:k + 1]
        acc_lo = acc_lo + gk * lo
        acc_hi = acc_hi + gk * hi
    ms = (jnp.sum(acc_lo * acc_lo, axis=-1, keepdims=True)
          + jnp.sum(acc_hi * acc_hi, axis=-1, keepdims=True)) / (2 * half)
    inv = lax.rsqrt(ms + EPS)
    o_ref[:, :half] = acc_lo * inv * g_ref[:, :half]
    o_ref[:, half:] = acc_hi * inv * g_ref[:, half:]

    @pl.when(i + 1 < n)
    def _():
        start_tile(dnext_ref, 1 - cur, n_groups // 2, n_groups)


def _combine(dest, y, x2, gate, g, rows):
    n, d = x2.shape
    nt = n // rows
    return pl.pallas_call(
        functools.partial(_combine_kernel, rows=rows),
        grid=(nt,),
        in_specs=[pl.BlockSpec((1, 1, rows * TOP_K), lambda i: (i, 0, 0), memory_space=pltpu.SMEM),
                  pl.BlockSpec((1, 1, rows * TOP_K), lambda i: (jnp.minimum(i + 1, nt - 1), 0, 0),
                               memory_space=pltpu.SMEM),
                  pl.BlockSpec(memory_space=pl.ANY),
                  pl.BlockSpec((rows, d), lambda i: (i, 0)),
                  pl.BlockSpec((rows, LANES), lambda i: (i, 0)),
                  pl.BlockSpec((1, d), lambda i: (0, 0))],
        out_specs=pl.BlockSpec((rows, d), lambda i: (i, 0)),
        out_shape=jax.ShapeDtypeStruct((n, d), F32),
        scratch_shapes=[pltpu.VMEM((2, TOP_K, rows * (d // 2 // LANES), LANES), U32),
                        pltpu.SemaphoreType.DMA((2,))],
        compiler_params=_params("arbitrary"),
        name="combine_norm",
    )(dest.reshape(nt, 1, rows * TOP_K), dest.reshape(nt, 1, rows * TOP_K), y, x2, gate,
      g.reshape(1, d))


def kernel(x, mem, norm_mix_g, w_in, rel_table, conv_w, conv_b, w_ga, b_ga, w_gx, b_gx, lru_lambda, norm_att_out_g, norm_lru_out_g, w_out, norm_cross_g, norm_mem_g, w_cq, w_ckv, w_co, norm_ffn_g, w_router, b_router, w_gu, b_gu, w_down, b_down, norm_final_g):
    b, seq, d = x.shape
    depth = w_in.shape[0]
    att_width = d // 2
    lru_width = d - att_width
    n_experts = w_router.shape[-1]
    n_tok = b * seq
    mem_len = mem.shape[1]
    assert depth == 1 and n_experts <= LANES and seq % min(ATT_QBLOCK, seq) == 0

    moe_rows = min(MOE_ROWS, n_tok)
    n_tiles = (n_tok * TOP_K + n_experts * (moe_rows - 1)) // moe_rows
    n_slots = n_tiles * moe_rows

    for l in range(depth):
        qkv, xy = _norm_proj(x.reshape(n_tok, d), norm_mix_g[l], w_in[l].astype(BF16),
                             [(3 * att_width, BF16), (2 * lru_width, F32)], min(PROJ_ROWS, seq))
        att = _attention(qkv.reshape(b, seq, 3 * att_width), rel_table[l], att_width)
        lru = _lru(xy.reshape(b, seq, 2 * lru_width), conv_w[l], conv_b[l], w_ga[l], b_ga[l],
                   w_gx[l], b_gx[l], lru_lambda[l], lru_width)
        (ckv,) = _norm_proj(mem.reshape(b * mem_len, d), norm_mem_g[l], w_ckv[l].astype(BF16),
                            [(w_ckv.shape[-1], BF16)], min(PROJ_ROWS, mem_len))
        ckv = ckv.reshape(b, mem_len, -1)
        wr_pad = jnp.zeros((d, LANES), F32).at[:, :n_experts].set(w_router[l]).astype(BF16)
        br_pad = jnp.full((1, LANES), NEG, F32).at[0, :n_experts].set(b_router[l])
        x2, hpack, route, gate, counts = _mid(
            x, att, lru, ckv, norm_att_out_g[l], norm_lru_out_g[l], w_out[l].astype(BF16),
            norm_cross_g[l], w_cq[l].astype(BF16), w_co[l].astype(BF16), norm_ffn_g[l],
            wr_pad, br_pad, n_experts, min(MID_ROWS, seq))
        counts = counts[0, :n_experts].astype(I32)
        padded = (counts + moe_rows - 1) // moe_rows * moe_rows
        pad_end = jnp.cumsum(padded)
        pad_start = pad_end - padded
        expert_ids = jnp.arange(n_experts, dtype=I32)
        sel_e = route[:, :TOP_K]
        start_of = jnp.sum(jnp.where(sel_e[:, :, None] == expert_ids, pad_start, 0), axis=-1)
        dest = start_of + route[:, TOP_K:2 * TOP_K]
        total = pad_end[-1]
        meta = (total // moe_rows).astype(I32).reshape(1)
        tile_start = jnp.minimum(jnp.arange(n_tiles, dtype=I32) * moe_rows, total - moe_rows)
        tile_expert = jnp.sum(tile_start[:, None] >= pad_end[None, :], axis=1).astype(I32)
        later = (expert_ids[None, :] > expert_ids[:, None]) & (padded[None, :] > 0)
        next_expert = jnp.min(jnp.where(later, expert_ids[None, :], n_experts), axis=1)
        next_expert = jnp.where(next_expert == n_experts, -1, next_expert)
        of_tile = lambda table: jnp.sum(
            jnp.where(tile_expert[:, None] == expert_ids, table, 0), axis=1).astype(I32)
        tile_next = of_tile(next_expert)
        in_use = jnp.arange(n_tiles, dtype=I32) * moe_rows < total
        tile_valid = jnp.where(in_use, jnp.clip(
            of_tile(pad_start + counts) - tile_start, 0, moe_rows), 0).astype(I32)
        tiles = (tile_expert, tile_next, tile_valid)
        fill = jnp.concatenate([pad_start + counts, total[None], padded - counts,
                                ((n_slots - total) // (moe_rows // 2))[None]]).astype(I32)
        xs = _dispatch(fill, dest, hpack, n_slots, min(DISPATCH_ROWS, n_tok), moe_rows,
                       d // 2 // LANES)
        cols = min(MOE_COLS, w_down.shape[2] // 2)
        act = _grouped_matmul(tiles, meta, xs, w_gu[l], b_gu[l], moe_rows, cols, True)
        y = _grouped_matmul(tiles, meta, act, w_down[l], b_down[l], moe_rows,
                            min(MOE_COLS, d // 2), False)
        x = _combine(dest, y, x2, gate, norm_final_g, min(COMBINE_ROWS, n_tok)).reshape(b, seq, d)
    return x
```

```python
import functools

import jax
import jax.numpy as jnp
from jax import lax
from jax.experimental import pallas as pl
from jax.experimental.pallas import tpu as pltpu

F32 = jnp.float32
BF16 = jnp.bfloat16
U32 = jnp.uint32
I32 = jnp.int32

EPS = 1e-6
CHUNK = 64
LEFT_CHUNKS = 8
ATT_HEAD_DIM = 64
MAX_REL = 2 * CHUNK
CONV_WIDTH = 4
LRU_C = 8.0
X_HEADS = 4
TOP_K = 4
SWIGLU_ALPHA = 1.702
SWIGLU_LIMIT = 7.0
NEG = -1e30
TINY = 1e-37

LANES = 128
SUBLANES = 8
MXU_COLS = 256
DMA_PRIORITIES = 2
VMEM_LIMIT = 56 * 1024 * 1024

PROJ_ROWS = 512
ATT_QBLOCK = 256
MID_ROWS = 512
MOE_ROWS = 1024
MOE_COLS = 1024
CAST_ROWS = 64
MOE_PART_ROWS = (128, 512)
DISPATCH_ROWS = 256
DISPATCH_GROUP = 8
COMBINE_ROWS = 256
COMBINE_GROUP = 8


def _rms(x, g):
    ms = jnp.mean(x * x, axis=-1, keepdims=True)
    return x * lax.rsqrt(ms + EPS) * g


def _fold_lanes(x, op):
    acc = x[:, :LANES]
    for c in range(LANES, x.shape[-1], LANES):
        acc = op(acc, x[:, c:c + LANES])
    return acc


def _params(*sem):
    return pltpu.CompilerParams(dimension_semantics=sem, vmem_limit_bytes=VMEM_LIMIT)


def _norm_proj_kernel(x_ref, g_ref, w_ref, *out_refs, col_chunk):
    h = _rms(x_ref[...], g_ref[...]).astype(BF16)
    c0 = 0
    for o_ref in out_refs:
        n = o_ref.shape[-1]
        for s in range(0, n, col_chunk):
            e = min(s + col_chunk, n)
            o_ref[:, s:e] = jnp.dot(h, w_ref[:, c0 + s:c0 + e],
                                    preferred_element_type=F32).astype(o_ref.dtype)
        c0 += n


def _norm_proj(x2d, g, w_bf, outs, rows):
    m, d = x2d.shape
    n_total = w_bf.shape[1]
    assert sum(n for n, _ in outs) == n_total and m % rows == 0
    return pl.pallas_call(
        functools.partial(_norm_proj_kernel, col_chunk=512),
        grid=(m // rows,),
        in_specs=[pl.BlockSpec((rows, d), lambda i: (i, 0)),
                  pl.BlockSpec((1, d), lambda i: (0, 0)),
                  pl.BlockSpec((d, n_total), lambda i: (0, 0), pipeline_mode=pl.Buffered(1))],
        out_specs=[pl.BlockSpec((rows, n), lambda i: (i, 0)) for n, _ in outs],
        out_shape=[jax.ShapeDtypeStruct((m, n), dt) for n, dt in outs],
        compiler_params=_params("parallel"),
        name="norm_proj",
    )(x2d, g.reshape(1, d), w_bf)


def _attn_kernel(q_ref, k_ref, v_ref, bias_ref, o_ref, kpad, vpad, *, seq, pad, qb, kb):
    hd = ATT_HEAD_DIM
    kpad[0:pad, :] = jnp.zeros((pad, LANES), BF16)
    vpad[0:pad, :] = jnp.zeros((pad, LANES), BF16)
    kpad[pad:pad + seq, :] = k_ref[0]
    vpad[pad:pad + seq, :] = v_ref[0]
    scale = hd ** -0.5
    kcol = lax.broadcasted_iota(I32, (qb, kb), 1)
    head_of_lane = lax.broadcasted_iota(I32, (qb, LANES), 1) // hd

    def block(ib, carry, *, near_start):
        s0 = pl.multiple_of(ib * qb, qb)
        q = q_ref[0, pl.ds(s0, qb), :] * scale
        kblk = kpad[pl.ds(s0, kb), :]
        vblk = vpad[pl.ds(s0, kb), :]
        out = None
        for hh in range(LANES // hd):
            mine = head_of_lane == hh
            s = lax.dot_general(jnp.where(mine, q, 0.0), kblk, (((1,), (1,)), ((), ())),
                                preferred_element_type=F32)
            s = s + bias_ref[hh]
            if near_start:
                s = jnp.where(kcol < (pad - s0), NEG, s)
            m = jnp.max(_fold_lanes(s, jnp.maximum), axis=-1, keepdims=True)
            p = jnp.exp(s - m)
            l = jnp.sum(_fold_lanes(p, jnp.add), axis=-1, keepdims=True)
            o = jnp.dot(p.astype(BF16), vblk, preferred_element_type=F32) / l
            out = o if out is None else jnp.where(mine, o, out)
        o_ref[0, pl.ds(s0, qb), :] = out.astype(o_ref.dtype)
        return carry

    n_near = min(pad // qb, seq // qb)
    lax.fori_loop(0, n_near, functools.partial(block, near_start=True), 0, unroll=2)
    lax.fori_loop(n_near, seq // qb, functools.partial(block, near_start=False), 0, unroll=True)


def _attention(qkv, rel_table, att_width):
    b, seq, _ = qkv.shape
    heads = att_width // ATT_HEAD_DIM
    hp = LANES // ATT_HEAD_DIM
    qb = min(ATT_QBLOCK, seq)
    pad = LEFT_CHUNKS * CHUNK
    kb = qb + pad
    ql = jnp.arange(qb)[:, None]
    kl = jnp.arange(kb)[None, :]
    span = qb + kb - 1
    rel_u = (qb - 1 + pad) - jnp.arange(span)
    u = rel_table[:, jnp.clip(rel_u, -MAX_REL, MAX_REL) + MAX_REL].astype(F32)
    u = jnp.concatenate([u, jnp.zeros((heads, 1), F32)], axis=1)
    skew = jnp.tile(u, (1, qb))[:, :qb * span].reshape(heads, qb, span)
    bias = skew[:, :, qb - 1:qb - 1 + kb]
    dchunk = ql // CHUNK + LEFT_CHUNKS - kl // CHUNK
    band = (dchunk >= 0) & (dchunk <= LEFT_CHUNKS)
    bias = jnp.where(band[None], bias, NEG)
    nblk = att_width // LANES
    return pl.pallas_call(
        functools.partial(_attn_kernel, seq=seq, pad=pad, qb=qb, kb=kb),
        grid=(heads // hp, b),
        in_specs=[pl.BlockSpec((1, seq, LANES), lambda j, i: (i, 0, j)),
                  pl.BlockSpec((1, seq, LANES), lambda j, i: (i, 0, nblk + j)),
                  pl.BlockSpec((1, seq, LANES), lambda j, i: (i, 0, 2 * nblk + j)),
                  pl.BlockSpec((hp, qb, kb), lambda j, i: (j, 0, 0))],
        out_specs=pl.BlockSpec((1, seq, LANES), lambda j, i: (i, 0, j)),
        out_shape=jax.ShapeDtypeStruct((b, seq, att_width), BF16),
        scratch_shapes=[pltpu.VMEM((seq + pad, LANES), BF16),
                        pltpu.VMEM((seq + pad, LANES), BF16)],
        compiler_params=_params("parallel", "parallel"),
        name="band_attention",
    )(qkv, qkv, qkv, bias)


def _lru_kernel(xr_ref, yg_ref, cw_ref, cb_ref, wg_ref, bg_ref, lam_ref, o_ref,
                a_scr, b_scr, h_scr, *, seq):
    bw = xr_ref.shape[-1]
    x = xr_ref[0]
    row = lax.broadcasted_iota(I32, (seq, bw), 0)
    cw = cw_ref[...]
    xc = cb_ref[...] + cw[CONV_WIDTH - 1:CONV_WIDTH, :] * x
    for j in range(1, CONV_WIDTH):
        xs = jnp.where(row >= j, pltpu.roll(x, j, 0), 0.0)
        xc = xc + cw[CONV_WIDTH - 1 - j:CONV_WIDTH - j, :] * xs
    gates = jnp.dot(xc.astype(BF16), wg_ref[0], preferred_element_type=F32) + bg_ref[0]
    r = 1.0 / (1.0 + jnp.exp(-gates[:, :bw]))
    i = 1.0 / (1.0 + jnp.exp(-gates[:, bw:]))
    z = -lam_ref[...]
    softplus = jnp.maximum(z, 0.0) + jnp.log1p(jnp.exp(-jnp.abs(z)))
    a = jnp.exp(-LRU_C * r * softplus)
    v = 1.0 - a * a
    b = v * lax.rsqrt(jnp.maximum(v, TINY)) * (i * xc)
    a_scr[...] = a
    b_scr[...] = b
    ng = seq // SUBLANES
    ga, gb = [], []
    for j in range(SUBLANES):
        aj = a_scr[pl.ds(j, ng, stride=SUBLANES), :]
        bj = b_scr[pl.ds(j, ng, stride=SUBLANES), :]
        if j:
            bj = aj * gb[-1] + bj
            aj = aj * ga[-1]
        ga.append(aj)
        gb.append(bj)
    ta, tb = ga[-1], gb[-1]
    grow = lax.broadcasted_iota(I32, (ng, bw), 0)
    k = 1
    while k < ng:
        keep = grow >= k
        tb = jnp.where(keep, ta * pltpu.roll(tb, k, 0) + tb, tb)
        ta = jnp.where(keep, ta * pltpu.roll(ta, k, 0), ta)
        k *= 2
    h_in = jnp.where(grow >= 1, pltpu.roll(tb, 1, 0), 0.0)
    for j in range(SUBLANES):
        h_scr[pl.ds(j, ng, stride=SUBLANES), :] = ga[j] * h_in + gb[j]
    g = yg_ref[0]
    gelu = 0.5 * g * (1.0 + jnp.tanh(0.7978845608028654 * (g + 0.044715 * (g * g * g))))
    o_ref[0] = (h_scr[...] * gelu).astype(o_ref.dtype)


def _lru(xy, conv_w, conv_b, w_ga, b_ga, w_gx, b_gx, lam, lru_width):
    b, seq, _ = xy.shape
    nb, bw, _ = w_ga.shape
    wg = jnp.concatenate([w_ga, w_gx], axis=-1).astype(BF16)
    bg = jnp.concatenate([b_ga, b_gx], axis=-1).reshape(nb, 1, 2 * bw)
    return pl.pallas_call(
        functools.partial(_lru_kernel, seq=seq),
        grid=(b, nb),
        in_specs=[pl.BlockSpec((1, seq, bw), lambda i, n: (i, 0, n)),
                  pl.BlockSpec((1, seq, bw), lambda i, n: (i, 0, nb + n)),
                  pl.BlockSpec((CONV_WIDTH, bw), lambda i, n: (0, n)),
                  pl.BlockSpec((1, bw), lambda i, n: (0, n)),
                  pl.BlockSpec((1, bw, 2 * bw), lambda i, n: (n, 0, 0)),
                  pl.BlockSpec((1, 1, 2 * bw), lambda i, n: (n, 0, 0)),
                  pl.BlockSpec((1, bw), lambda i, n: (0, n))],
        out_specs=pl.BlockSpec((1, seq, bw), lambda i, n: (i, 0, n)),
        out_shape=jax.ShapeDtypeStruct((b, seq, lru_width), BF16),
        scratch_shapes=[pltpu.VMEM((seq, bw), F32)] * 3,
        compiler_params=_params("parallel", "parallel"),
        name="rg_lru",
    )(xy, xy, conv_w, conv_b.reshape(1, -1), wg, bg, lam.reshape(1, -1))


def _pack_bf16_pair(lo, hi):
    lo_b = pltpu.bitcast(lo.astype(BF16).astype(F32), U32) >> 16
    hi_b = pltpu.bitcast(hi.astype(BF16).astype(F32), U32) & jnp.uint32(0xFFFF0000)
    return hi_b | lo_b


def _unpack_bf16_pair(w):
    lo = pltpu.bitcast(w << 16, F32)
    hi = pltpu.bitcast(w & jnp.uint32(0xFFFF0000), F32)
    return lo, hi


def _store_token_major(ref, x, nc=None):
    tokens = x.shape[0]
    nc = nc or x.shape[-1] // LANES
    for c in range(x.shape[-1] // LANES):
        ref[pl.ds(c, tokens, stride=nc), :] = x[:, c * LANES:(c + 1) * LANES]


def _load_token_major(load, tokens, nc):
    return jnp.concatenate([load(pl.ds(c, tokens, stride=nc)) for c in range(nc)], axis=-1)


def _mid_kernel(x_ref, att_ref, lru_ref, ck_ref, cv_ref, ga_ref, gl_ref, wo_ref, gc_ref, wcq_ref,
                wco_ref, gf_ref, wr_ref, br_ref,
                x2_ref, hp_ref, route_ref, gate_ref, cnt_ref, cnt_scr, *, n_experts):
    rows, d = x_ref.shape
    aw = att_ref.shape[-1]

    @pl.when((pl.program_id(0) == 0) & (pl.program_id(1) == 0))
    def _():
        cnt_scr[...] = jnp.zeros_like(cnt_scr)

    att_n = _rms(att_ref[...].astype(F32), ga_ref[...]).astype(BF16)
    lru_n = _rms(lru_ref[...].astype(F32), gl_ref[...]).astype(BF16)
    x1 = (x_ref[...]
          + jnp.dot(att_n, wo_ref[0:aw, :], preferred_element_type=F32)
          + jnp.dot(lru_n, wo_ref[aw:, :], preferred_element_type=F32))

    hq = _rms(x1, gc_ref[...]).astype(BF16)
    xw = wcq_ref.shape[-1]
    xhd = xw // X_HEADS
    cq = (jnp.dot(hq, wcq_ref[...], preferred_element_type=F32) * (xhd ** -0.5)).astype(BF16)
    ck = ck_ref[0]
    cv = cv_ref[0]
    heads = []
    for h in range(X_HEADS):
        sl = slice(h * xhd, (h + 1) * xhd)
        s = lax.dot_general(cq[:, sl], ck[:, sl], (((1,), (1,)), ((), ())),
                            preferred_element_type=F32)
        m = jnp.max(s, axis=-1, keepdims=True)
        p = jnp.exp(s - m)
        l = jnp.sum(p, axis=-1, keepdims=True)
        heads.append(jnp.dot(p.astype(BF16), cv[:, sl], preferred_element_type=F32) / l)
    o = jnp.concatenate(heads, axis=-1).astype(BF16)
    x2 = x1 + jnp.dot(o, wco_ref[...], preferred_element_type=F32)
    x2_ref[...] = x2

    hn = _rms(x2, gf_ref[...])
    _store_token_major(hp_ref, _pack_bf16_pair(hn[:, :d // 2], hn[:, d // 2:]))

    logits = jnp.dot(hn.astype(BF16), wr_ref[...], preferred_element_type=F32) + br_ref[...]
    lane = lax.broadcasted_iota(I32, (rows, LANES), 1).astype(F32)
    work = logits
    sel_e, sel_v = [], []
    onehot = jnp.zeros((rows, LANES), F32)
    for _ in range(TOP_K):
        v = jnp.max(work, axis=-1, keepdims=True)
        e = jnp.min(jnp.where(work == v, lane, float(LANES)), axis=-1, keepdims=True)
        hit = lane == e
        onehot = jnp.where(hit, 1.0, onehot)
        work = jnp.where(hit, NEG * 2, work)
        sel_e.append(e)
        sel_v.append(v)
    ex = [jnp.exp(v - sel_v[0]) for v in sel_v]
    den = ex[0] + ex[1] + ex[2] + ex[3]

    ri = lax.broadcasted_iota(I32, (rows, rows), 0)
    ci = lax.broadcasted_iota(I32, (rows, rows), 1)
    tri = jnp.where(ci < ri, 1.0, 0.0).astype(BF16)
    before = jnp.dot(tri, onehot.astype(BF16), preferred_element_type=F32) + cnt_scr[...]
    route = jnp.zeros((rows, LANES), F32)
    gate = jnp.zeros((rows, LANES), F32)
    for k in range(TOP_K):
        rank = jnp.sum(jnp.where(lane == sel_e[k], before, 0.0), axis=-1, keepdims=True)
        route = jnp.where(lane == float(k), sel_e[k], route)
        route = jnp.where(lane == float(TOP_K + k), rank, route)
        gate = jnp.where(lane == float(k), ex[k] / den, gate)
    route_ref[...] = route.astype(I32)
    gate_ref[...] = gate
    cnt_scr[...] = cnt_scr[...] + jnp.sum(onehot, axis=0, keepdims=True)
    cnt_ref[...] = cnt_scr[...]


def _mid(x, att, lru, ckv, ga, gl, wo_bf, gc, wcq_bf, wco_bf, gf, wr_pad, br_pad, n_experts, rows):
    b, seq, d = x.shape
    aw, lw = att.shape[-1], lru.shape[-1]
    mem_len, xw2 = ckv.shape[1], ckv.shape[2]
    xw = xw2 // 2
    n = b * seq
    nt = seq // rows
    row_map = lambda i, t: (i * nt + t, 0)
    const = lambda i, t: (0, 0)
    res = lambda shape: pl.BlockSpec(shape, const, pipeline_mode=pl.Buffered(1))
    return pl.pallas_call(
        functools.partial(_mid_kernel, n_experts=n_experts),
        grid=(b, nt),
        in_specs=[pl.BlockSpec((rows, d), row_map),
                  pl.BlockSpec((rows, aw), row_map),
                  pl.BlockSpec((rows, lw), row_map),
                  pl.BlockSpec((1, mem_len, xw), lambda i, t: (i, 0, 0)),
                  pl.BlockSpec((1, mem_len, xw), lambda i, t: (i, 0, 1)),
                  res((1, aw)), res((1, lw)), res((aw + lw, d)), res((1, d)), res((d, xw)),
                  res((xw, d)), res((1, d)), res((d, LANES)), res((1, LANES))],
        out_specs=[pl.BlockSpec((rows, d), row_map),
                   pl.BlockSpec((rows * (d // 2 // LANES), LANES), row_map),
                   pl.BlockSpec((rows, LANES), row_map),
                   pl.BlockSpec((rows, LANES), row_map),
                   pl.BlockSpec((1, LANES), const)],
        out_shape=[jax.ShapeDtypeStruct((n, d), F32),
                   jax.ShapeDtypeStruct((n * (d // 2 // LANES), LANES), U32),
                   jax.ShapeDtypeStruct((n, LANES), I32),
                   jax.ShapeDtypeStruct((n, LANES), F32),
                   jax.ShapeDtypeStruct((1, LANES), F32)],
        scratch_shapes=[pltpu.VMEM((1, LANES), F32)],
        compiler_params=_params("arbitrary", "arbitrary"),
        name="mix_cross_router",
    )(x.reshape(n, d), att.reshape(n, aw), lru.reshape(n, lw), ckv, ckv,
      ga.reshape(1, aw), gl.reshape(1, lw), wo_bf, gc.reshape(1, d), wcq_bf, wco_bf,
      gf.reshape(1, d), wr_pad, br_pad)


def _dispatch_kernel(fill_ref, dest_ref, src_ref, xs_ref, buf, zbuf, lsem, ssem, zsem,
                     *, rows, nc, n_fill, tail_chunk):
    i = pl.program_id(0)
    n = pl.num_programs(0)
    ztok = zbuf.shape[0] // nc

    def tokens(first, count):
        return pl.ds(pl.multiple_of(first * nc, nc), count * nc)

    def load(t, slot):
        return pltpu.make_async_copy(src_ref.at[tokens(t * rows, rows), :], buf.at[slot],
                                     lsem.at[slot])

    def zero_copy(start, size):
        return pltpu.make_async_copy(zbuf.at[pl.ds(0, size * nc), :],
                                     xs_ref.at[tokens(start, size), :], zsem)

    def fill_chunks(fn):
        for e in range(n_fill):
            start = fill_ref[e]
            length = fill_ref[n_fill + 1 + e]
            size = ztok
            while size >= 1:
                part = length & size
                @pl.when(part != 0)
                def _(start=start, size=size):
                    fn(zero_copy(start, size))
                start = start + part
                size //= 2
        tail_start = fill_ref[n_fill]
        n_tail = fill_ref[2 * n_fill + 1]

        def tail(c, carry):
            fn(zero_copy(tail_start + c * tail_chunk, tail_chunk))
            return carry

        lax.fori_loop(0, n_tail, tail, 0)

    @pl.when(i == 0)
    def _():
        load(0, 0).start()
        zbuf[...] = jnp.zeros_like(zbuf)
        fill_chunks(lambda c: c.start())
        fill_chunks(lambda c: c.wait())

    @pl.when(i + 1 < n)
    def _():
        load(i + 1, (i + 1) % 3).start()

    slot = i % 3
    par = i % 2
    load(i, slot).wait()

    def start_rows(g, c):
        for j in range(DISPATCH_GROUP):
            tok = g * DISPATCH_GROUP + j
            for k in range(TOP_K):
                dst = dest_ref[0, 0, tok * TOP_K + k]
                pltpu.make_async_copy(buf.at[slot, tokens(tok, 1), :],
                                      xs_ref.at[tokens(dst, 1), :],
                                      ssem.at[par]).start(priority=k % DMA_PRIORITIES)
        return c

    lax.fori_loop(0, rows // DISPATCH_GROUP, start_rows, 0)

    def wait_tile(p):
        for _ in range(TOP_K):
            pltpu.make_async_copy(buf.at[0], xs_ref.at[pl.ds(0, rows * nc), :], ssem.at[p]).wait()

    @pl.when(i > 0)
    def _():
        wait_tile(1 - par)

    @pl.when(i == n - 1)
    def _():
        wait_tile(par)


def _dispatch(fill, dest, src, n_slots, rows, moe_rows, nc):
    n_tok = src.shape[0] // nc
    nt = n_tok // rows
    n_fill = (fill.shape[0] - 2) // 2
    ztok = moe_rows // 2
    return pl.pallas_call(
        functools.partial(_dispatch_kernel, rows=rows, nc=nc, n_fill=n_fill, tail_chunk=ztok),
        grid_spec=pltpu.PrefetchScalarGridSpec(
            num_scalar_prefetch=1,
            grid=(nt,),
            in_specs=[pl.BlockSpec((1, 1, rows * TOP_K), lambda i, fill: (i, 0, 0),
                                   memory_space=pltpu.SMEM),
                      pl.BlockSpec(memory_space=pl.ANY)],
            out_specs=pl.BlockSpec(memory_space=pl.ANY),
            scratch_shapes=[pltpu.VMEM((3, rows * nc, LANES), src.dtype),
                            pltpu.VMEM((ztok * nc, LANES), src.dtype),
                            pltpu.SemaphoreType.DMA((3,)),
                            pltpu.SemaphoreType.DMA((2,)),
                            pltpu.SemaphoreType.DMA(())]),
        out_shape=jax.ShapeDtypeStruct((n_slots * nc, LANES), src.dtype),
        compiler_params=_params("arbitrary"),
        name="dispatch_scatter",
    )(fill, dest.reshape(nt, 1, rows * TOP_K), src)


def _grouped_kernel(te_ref, nxt_ref, valid_ref, meta_ref, x_ref, w_ref, ba_ref, bb_ref, o_ref,
                    w_f32, wa_bf, wb_bf, wsem, *, swiglu):
    j = pl.program_id(0)
    i = pl.program_id(1)
    nj = pl.num_programs(0)
    cols = wa_bf.shape[-1]
    expert = te_ref[i]
    new_expert = (i == 0) | (expert != te_ref[jnp.maximum(i - 1, 0)])

    def weight_copy(jj, ee, which):
        col = pl.multiple_of((which * nj + jj) * cols, cols)
        return pltpu.make_async_copy(w_ref.at[ee, :, pl.ds(col, cols)], w_f32.at[which],
                                     wsem.at[which])

    def start_weights(jj, ee):
        weight_copy(jj, ee, 0).start()
        weight_copy(jj, ee, 1).start()

    @pl.when((j == 0) & (i == 0))
    def _():
        start_weights(0, expert)

    @pl.when(new_expert)
    def _():
        weight_copy(j, expert, 0).wait()
        weight_copy(j, expert, 1).wait()
        def cast_rows(c, carry):
            rs = pl.ds(pl.multiple_of(c * CAST_ROWS, CAST_ROWS), CAST_ROWS)
            wa_bf[rs, :] = w_f32[0, rs, :].astype(BF16)
            wb_bf[rs, :] = w_f32[1, rs, :].astype(BF16)
            return carry

        lax.fori_loop(0, wa_bf.shape[0] // CAST_ROWS, cast_rows, 0)
        following = nxt_ref[i]

        @pl.when(following >= 0)
        def _():
            start_weights(j, following)

        @pl.when((following < 0) & (j + 1 < nj))
        def _():
            start_weights(j + 1, te_ref[0])

    tile_rows = o_ref.shape[0] if swiglu else x_ref.shape[0]
    nc_in = x_ref.shape[0] // tile_rows
    nc_out = o_ref.shape[0] // tile_rows

    def compute(rows):
        if swiglu:
            lo, hi = _unpack_bf16_pair(_load_token_major(lambda s: x_ref[s, :], rows, nc_in))
            xb = jnp.concatenate([lo.astype(BF16), hi.astype(BF16)], axis=-1)
        else:
            xb = x_ref[0:rows, :]
        for c0 in range(0, cols, MXU_COLS):
            cs = slice(c0, c0 + MXU_COLS)
            a = jnp.dot(xb, wa_bf[:, cs], preferred_element_type=F32) + ba_ref[0, :, cs]
            b = jnp.dot(xb, wb_bf[:, cs], preferred_element_type=F32) + bb_ref[0, :, cs]
            if swiglu:
                gate = jnp.minimum(a, SWIGLU_LIMIT)
                up = jnp.clip(b, -SWIGLU_LIMIT, SWIGLU_LIMIT)
                act = (up + 1.0) * gate * (1.0 / (1.0 + jnp.exp(-SWIGLU_ALPHA * gate)))
                o_ref[0:rows, cs] = act.astype(o_ref.dtype)
            else:
                packed = _pack_bf16_pair(a, b)
                for c in range(MXU_COLS // LANES):
                    o_ref[pl.ds(c0 // LANES + c, rows, stride=nc_out), :] = (
                        packed[:, c * LANES:(c + 1) * LANES])
        if rows < tile_rows:
            rest = o_ref.shape[0] // tile_rows * rows
            o_ref[rest:, :] = jnp.zeros((o_ref.shape[0] - rest, o_ref.shape[1]), o_ref.dtype)

    valid = valid_ref[i]
    below = 0
    for part in [p for p in MOE_PART_ROWS if p < tile_rows] + [tile_rows]:
        @pl.when((valid > below) & (valid <= part))
        def _(part=part):
            compute(part)
        below = part

    @pl.when(valid == 0)
    def _():
        o_ref[...] = jnp.zeros_like(o_ref)


def _grouped_matmul(tiles, meta, x, w, bias, rows, cols, swiglu):
    n_exp, kdim, n2 = w.shape
    half = n2 // 2
    nj = half // cols
    used = lambda i, meta: jnp.minimum(i, meta[0] - 1)
    if swiglu:
        x_rows = rows * (kdim // 2 // LANES)
        n_slots = x.shape[0] // (kdim // 2 // LANES)
        out_spec = pl.BlockSpec((rows, cols), lambda j, i, te, nx, vl, meta: (i, j))
        out_shape = jax.ShapeDtypeStruct((n_slots, half), BF16)
    else:
        assert nj == 1
        x_rows = rows
        n_slots = x.shape[0]
        out_spec = pl.BlockSpec((rows * (half // LANES), LANES), lambda j, i, te, nx, vl, meta: (i, 0))
        out_shape = jax.ShapeDtypeStruct((n_slots * (half // LANES), LANES), U32)
    n_tiles = n_slots // rows
    return pl.pallas_call(
        functools.partial(_grouped_kernel, swiglu=swiglu),
        grid_spec=pltpu.PrefetchScalarGridSpec(
            num_scalar_prefetch=4,
            grid=(nj, n_tiles),
            in_specs=[pl.BlockSpec((x_rows, x.shape[1]),
                                   lambda j, i, te, nx, vl, meta: (used(i, meta), 0)),
                      pl.BlockSpec(memory_space=pl.ANY),
                      pl.BlockSpec((1, 1, cols), lambda j, i, te, nx, vl, meta: (te[i], 0, j)),
                      pl.BlockSpec((1, 1, cols), lambda j, i, te, nx, vl, meta: (te[i], 0, nj + j))],
            out_specs=out_spec,
            scratch_shapes=[pltpu.VMEM((2, kdim, cols), F32),
                            pltpu.VMEM((kdim, cols), BF16), pltpu.VMEM((kdim, cols), BF16),
                            pltpu.SemaphoreType.DMA((2,))]),
        out_shape=out_shape,
        compiler_params=_params("arbitrary", "arbitrary"),
        name="moe_gate_up" if swiglu else "moe_down",
    )(*tiles, meta, x, w, bias.reshape(n_exp, 1, n2), bias.reshape(n_exp, 1, n2))


def _combine_kernel(dcur_ref, dnext_ref, y_ref, x2_ref, gate_ref, g_ref, o_ref, ybuf, sem, *, rows):
    i = pl.program_id(0)
    n = pl.num_programs(0)

    half = x2_ref.shape[-1] // 2
    nc = half // LANES
    cur = i % 2

    def token(t):
        return pl.ds(pl.multiple_of(t * nc, nc), nc)

    n_groups = rows // COMBINE_GROUP

    def start_tile(d_ref, slot, first_group, last_group):
        def step(g, c):
            for j in range(COMBINE_GROUP):
                tok = g * COMBINE_GROUP + j
                for k in range(TOP_K):
                    src = d_ref[0, 0, tok * TOP_K + k]
                    pltpu.make_async_copy(y_ref.at[token(src), :], ybuf.at[slot, k, token(tok), :],
                                          sem.at[slot]).start(priority=k % DMA_PRIORITIES)
            return c

        lax.fori_loop(first_group, last_group, step, 0)

    @pl.when(i == 0)
    def _():
        start_tile(dcur_ref, 0, 0, n_groups)

    @pl.when(i + 1 < n)
    def _():
        start_tile(dnext_ref, 1 - cur, 0, n_groups // 2)

    for k in range(TOP_K):
        pltpu.make_async_copy(y_ref.at[pl.ds(0, rows * nc), :], ybuf.at[cur, k],
                              sem.at[cur]).wait()

    gate = gate_ref[...]
    acc_lo = x2_ref[:, :half]
    acc_hi = x2_ref[:, half:]
    for k in range(TOP_K):
        lo, hi = _unpack_bf16_pair(
            _load_token_major(lambda s, k=k: ybuf[cur, k, s, :], rows, nc))
        gk = gate[:, k:k + 1]
        acc_lo = acc_lo + gk * lo
        acc_hi = acc_hi + gk * hi
    ms = (jnp.sum(acc_lo * acc_lo, axis=-1, keepdims=True)
          + jnp.sum(acc_hi * acc_hi, axis=-1, keepdims=True)) / (2 * half)
    inv = lax.rsqrt(ms + EPS)
    o_ref[:, :half] = acc_lo * inv * g_ref[:, :half]
    o_ref[:, half:] = acc_hi * inv * g_ref[:, half:]

    @pl.when(i + 1 < n)
    def _():
        start_tile(dnext_ref, 1 - cur, n_groups // 2, n_groups)


def _combine(dest, y, x2, gate, g, rows):
    n, d = x2.shape
    nt = n // rows
    return pl.pallas_call(
        functools.partial(_combine_kernel, rows=rows),
        grid=(nt,),
        in_specs=[pl.BlockSpec((1, 1, rows * TOP_K), lambda i: (i, 0, 0), memory_space=pltpu.SMEM),
                  pl.BlockSpec((1, 1, rows * TOP_K), lambda i: (jnp.minimum(i + 1, nt - 1), 0, 0),
                               memory_space=pltpu.SMEM),
                  pl.BlockSpec(memory_space=pl.ANY),
                  pl.BlockSpec((rows, d), lambda i: (i, 0)),
                  pl.BlockSpec((rows, LANES), lambda i: (i, 0)),
                  pl.BlockSpec((1, d), lambda i: (0, 0))],
        out_specs=pl.BlockSpec((rows, d), lambda i: (i, 0)),
        out_shape=jax.ShapeDtypeStruct((n, d), F32),
        scratch_shapes=[pltpu.VMEM((2, TOP_K, rows * (d // 2 // LANES), LANES), U32),
                        pltpu.SemaphoreType.DMA((2,))],
        compiler_params=_params("arbitrary"),
        name="combine_norm",
    )(dest.reshape(nt, 1, rows * TOP_K), dest.reshape(nt, 1, rows * TOP_K), y, x2, gate,
      g.reshape(1, d))


def kernel(x, mem, norm_mix_g, w_in, rel_table, conv_w, conv_b, w_ga, b_ga, w_gx, b_gx, lru_lambda, norm_att_out_g, norm_lru_out_g, w_out, norm_cross_g, norm_mem_g, w_cq, w_ckv, w_co, norm_ffn_g, w_router, b_router, w_gu, b_gu, w_down, b_down, norm_final_g):
    b, seq, d = x.shape
    depth = w_in.shape[0]
    att_width = d // 2
    lru_width = d - att_width
    n_experts = w_router.shape[-1]
    n_tok = b * seq
    mem_len = mem.shape[1]
    assert depth == 1 and n_experts <= LANES and seq % min(ATT_QBLOCK, seq) == 0

    moe_rows = min(MOE_ROWS, n_tok)
    n_tiles = (n_tok * TOP_K + n_experts * (moe_rows - 1)) // moe_rows
    n_slots = n_tiles * moe_rows

    for l in range(depth):
        qkv, xy = _norm_proj(x.reshape(n_tok, d), norm_mix_g[l], w_in[l].astype(BF16),
                             [(3 * att_width, BF16), (2 * lru_width, F32)], min(PROJ_ROWS, seq))
        att = _attention(qkv.reshape(b, seq, 3 * att_width), rel_table[l], att_width)
        lru = _lru(xy.reshape(b, seq, 2 * lru_width), conv_w[l], conv_b[l], w_ga[l], b_ga[l],
                   w_gx[l], b_gx[l], lru_lambda[l], lru_width)
        (ckv,) = _norm_proj(mem.reshape(b * mem_len, d), norm_mem_g[l], w_ckv[l].astype(BF16),
                            [(w_ckv.shape[-1], BF16)], min(PROJ_ROWS, mem_len))
        ckv = ckv.reshape(b, mem_len, -1)
        wr_pad = jnp.zeros((d, LANES), F32).at[:, :n_experts].set(w_router[l]).astype(BF16)
        br_pad = jnp.full((1, LANES), NEG, F32).at[0, :n_experts].set(b_router[l])
        x2, hpack, route, gate, counts = _mid(
            x, att, lru, ckv, norm_att_out_g[l], norm_lru_out_g[l], w_out[l].astype(BF16),
            norm_cross_g[l], w_cq[l].astype(BF16), w_co[l].astype(BF16), norm_ffn_g[l],
            wr_pad, br_pad, n_experts, min(MID_ROWS, seq))
        counts = counts[0, :n_experts].astype(I32)
        padded = (counts + moe_rows - 1) // moe_rows * moe_rows
        pad_end = jnp.cumsum(padded)
        pad_start = pad_end - padded
        expert_ids = jnp.arange(n_experts, dtype=I32)
        sel_e = route[:, :TOP_K]
        start_of = jnp.sum(jnp.where(sel_e[:, :, None] == expert_ids, pad_start, 0), axis=-1)
        dest = start_of + route[:, TOP_K:2 * TOP_K]
        total = pad_end[-1]
        meta = (total // moe_rows).astype(I32).reshape(1)
        tile_start = jnp.minimum(jnp.arange(n_tiles, dtype=I32) * moe_rows, total - moe_rows)
        tile_expert = jnp.sum(tile_start[:, None] >= pad_end[None, :], axis=1).astype(I32)
        later = (expert_ids[None, :] > expert_ids[:, None]) & (padded[None, :] > 0)
        next_expert = jnp.min(jnp.where(later, expert_ids[None, :], n_experts), axis=1)
        next_expert = jnp.where(next_expert == n_experts, -1, next_expert)
        of_tile = lambda table: jnp.sum(
            jnp.where(tile_expert[:, None] == expert_ids, table, 0), axis=1).astype(I32)
        tile_next = of_tile(next_expert)
        in_use = jnp.arange(n_tiles, dtype=I32) * moe_rows < total
        tile_valid = jnp.where(in_use, jnp.clip(
            of_tile(pad_start + counts) - tile_start, 0, moe_rows), 0).astype(I32)
        tiles = (tile_expert, tile_next, tile_valid)
        fill = jnp.concatenate([pad_start + counts, total[None], padded - counts,
                                ((n_slots - total) // (moe_rows // 2))[None]]).astype(I32)
        xs = _dispatch(fill, dest, hpack, n_slots, min(DISPATCH_ROWS, n_tok), moe_rows,
                       d // 2 // LANES)
        cols = min(MOE_COLS, w_down.shape[2] // 2)
        act = _grouped_matmul(tiles, meta, xs, w_gu[l], b_gu[l], moe_rows, cols, True)
        y = _grouped_matmul(tiles, meta, act, w_down[l], b_down[l], moe_rows,
                            min(MOE_COLS, d // 2), False)
        x = _combine(dest, y, x2, gate, norm_final_g, min(COMBINE_ROWS, n_tok)).reshape(b, seq, d)
    return x
```
